```python
import math
import jax, jax.numpy as jnp
from jax import lax
import numpy as np

D_MODEL = 1024
BATCH = 8
SEQ = 2048
DEPTH = 2

MEM_LEN = 256
EPS = 1e-6

SB_HEADS = 8
SB_HD = 64
SB_W = SB_HEADS * SB_HD
SB_BLOCK = 128

DN_HEADS = 4
DN_HD = 128
DN_W = DN_HEADS * DN_HD
DN_CONV = 4
DN_CHUNK = 64

RET_HEADS = 4
RET_QK_HD = 64
RET_V_HD = 128
RET_QK_W = RET_HEADS * RET_QK_HD
RET_V_W = RET_HEADS * RET_V_HD
RET_CHUNK = 128
ROPE_BASE = 10000.0

MEM_HEADS = 4
MEM_HD = 64
MEM_W = MEM_HEADS * MEM_HD

N_BRANCH = 4

IN_SPLITS = (SB_W, SB_W, SB_W, SB_W,
             DN_W, DN_W, DN_W, DN_W, DN_HEADS, DN_HEADS,
             RET_QK_W, RET_QK_W, RET_V_W, RET_V_W,
             MEM_W,
             N_BRANCH * D_MODEL)
IN_COLS = sum(IN_SPLITS)

kernel_name = "hybrid_gated_parallel_mixers"


def rmsnorm(x, g):
    xf = x.astype(jnp.float32)
    y = xf * lax.rsqrt(jnp.mean(xf * xf, axis=-1, keepdims=True) + EPS) * g.astype(jnp.float32)
    return y.astype(x.dtype)


def l2norm(x):
    return x * lax.rsqrt(jnp.sum(x * x, axis=-1, keepdims=True) + EPS)


def split_heads(t, n_heads):
    b, s, _ = t.shape
    return t.reshape(b, s, n_heads, -1).transpose(0, 2, 1, 3)


def merge_heads(t):
    b, h, s, d = t.shape
    return t.transpose(0, 2, 1, 3).reshape(b, s, h * d)


def stick_breaking_attention(q, k, v):
    s_len = q.shape[2]
    scale = SB_HD ** -0.5
    qf, kf, vf = q.astype(jnp.float32), k.astype(jnp.float32), v.astype(jnp.float32)
    outs = []
    for i in range(s_len // SB_BLOCK):
        q0, q1 = i * SB_BLOCK, (i + 1) * SB_BLOCK
        z = jnp.einsum('bhqd,bhkd->bhqk', qf[:, :, q0:q1], kf[:, :, :q1]) * scale
        t_idx = q0 + jnp.arange(SB_BLOCK)[:, None]
        s_idx = jnp.arange(q1)[None, :]
        causal = s_idx < t_idx
        log_stay = jnp.where(causal, jax.nn.log_sigmoid(-z), 0.0)
        after = lax.cumsum(log_stay, axis=3, reverse=True) - log_stay
        w = jnp.where(causal, jnp.exp(jax.nn.log_sigmoid(z) + after), 0.0)
        outs.append(jnp.einsum('bhqk,bhkd->bhqd', w, vf[:, :, :q1]))
    return jnp.concatenate(outs, axis=2).astype(q.dtype)


def causal_depthwise_conv(x, w):
    c = x.shape[-1]
    return lax.conv_general_dilated(
        x, w[:, None, :].astype(x.dtype), window_strides=(1,), padding=[(DN_CONV - 1, 0)],
        dimension_numbers=('NWC', 'WIO', 'NWC'), feature_group_count=c)


def gated_delta_rule(q, k, v, g, beta):
    out_dtype = v.dtype
    q, k, v = q.astype(jnp.float32), k.astype(jnp.float32), v.astype(jnp.float32)
    b, h, s_len, dk = q.shape
    dv = v.shape[-1]
    c = DN_CHUNK
    n = s_len // c
    q = q * dk ** -0.5

    def chunk(t):
        return t.reshape(b, h, n, c, *t.shape[3:])

    q, k, v, g, beta = chunk(q), chunk(k), chunk(v), chunk(g), chunk(beta)
    gc = jnp.cumsum(g, axis=-1)
    tril = jnp.tril(jnp.ones((c, c), bool))
    strict = jnp.tril(jnp.ones((c, c), bool), -1)
    decay = jnp.exp(jnp.where(tril, gc[..., :, None] - gc[..., None, :], -jnp.inf))
    k_beta = k * beta[..., None]
    v_beta = v * beta[..., None]
    lower = jnp.where(strict, jnp.einsum('bhncd,bhnmd->bhncm', k_beta, k) * decay, 0.0)
    eye = jnp.eye(c, dtype=jnp.float32)
    t_inv = lax.linalg.triangular_solve(eye + lower, jnp.broadcast_to(eye, lower.shape),
                                        left_side=True, lower=True, unit_diagonal=True)
    u = jnp.einsum('bhncm,bhnme->bhnce', t_inv, v_beta)
    w = jnp.einsum('bhncm,bhnmd->bhncd', t_inv, k_beta * jnp.exp(gc)[..., None])
    a_intra = jnp.where(tril, jnp.einsum('bhncd,bhnmd->bhncm', q, k) * decay, 0.0)

    def step(state, xs):
        q_c, k_c, u_c, w_c, a_c, g_c = xs
        v_new = u_c - jnp.einsum('bhcd,bhde->bhce', w_c, state)
        o = jnp.einsum('bhcd,bhde->bhce', q_c * jnp.exp(g_c)[..., None], state) \
            + jnp.einsum('bhcm,bhme->bhce', a_c, v_new)
        g_last = g_c[..., -1]
        state = state * jnp.exp(g_last)[..., None, None] + jnp.einsum(
            'bhcd,bhce->bhde', k_c * jnp.exp(g_last[..., None] - g_c)[..., None], v_new)
        return state, o

    xs = tuple(jnp.moveaxis(t, 2, 0) for t in (q, k, u, w, a_intra, gc))
    _, o = lax.scan(step, jnp.zeros((b, h, dk, dv), jnp.float32), xs)
    o = jnp.moveaxis(o, 0, 2).reshape(b, h, s_len, dv)
    return o.astype(out_dtype)


def rope(x, positions):
    half = x.shape[-1] // 2
    inv = ROPE_BASE ** (-jnp.arange(half, dtype=jnp.float32) / half)
    ang = positions[:, None, :, None].astype(jnp.float32) * inv
    cos, sin = jnp.cos(ang), jnp.sin(ang)
    xf = x.astype(jnp.float32)
    x1, x2 = xf[..., :half], xf[..., half:]
    return jnp.concatenate([x1 * cos - x2 * sin, x1 * sin + x2 * cos], axis=-1).astype(x.dtype)


def retention_chunkwise(q, k, v):
    out_dtype = v.dtype
    q, k, v = q.astype(jnp.float32), k.astype(jnp.float32), v.astype(jnp.float32)
    b, h, s_len, dk = q.shape
    dv = v.shape[-1]
    c = RET_CHUNK
    n = s_len // c
    k = k * dk ** -0.5
    log_gamma = jnp.log1p(-(2.0 ** (-5.0 - jnp.arange(h, dtype=jnp.float32))))
    idx = jnp.arange(c, dtype=jnp.float32)
    rel = idx[:, None] - idx[None, :]
    intra_decay = jnp.where(rel >= 0, jnp.exp(jnp.maximum(rel, 0.0) * log_gamma[:, None, None]), 0.0)
    cross_decay = jnp.exp((idx + 1.0) * log_gamma[:, None])
    state_decay = jnp.exp((c - 1.0 - idx) * log_gamma[:, None])
    chunk_decay = jnp.exp(c * log_gamma)

    def chunk(t):
        return jnp.moveaxis(t.reshape(b, h, n, c, t.shape[-1]), 2, 0)

    qc, kc, vc = chunk(q), chunk(k), chunk(v)
    scores = jnp.einsum('nbhcd,nbhmd->nbhcm', qc, kc) * intra_decay[None, None]
    o_intra = jnp.einsum('nbhcm,nbhme->nbhce', scores, vc)

    def step(state, xs):
        q_c, k_c, v_c = xs
        o_cross = jnp.einsum('bhcd,bhde->bhce', q_c, state) * cross_decay[None, :, :, None]
        state = state * chunk_decay[None, :, None, None] + jnp.einsum(
            'bhcd,bhce->bhde', k_c * state_decay[None, :, :, None], v_c)
        return state, o_cross

    _, o_cross = lax.scan(step, jnp.zeros((b, h, dk, dv), jnp.float32), (qc, kc, vc))
    o = jnp.moveaxis(o_intra + o_cross, 0, 2).reshape(b, h, s_len, dv)
    return o.astype(out_dtype)


def head_groupnorm(o, g):
    of = o.astype(jnp.float32)
    mu = jnp.mean(of, axis=-1, keepdims=True)
    var = jnp.mean(jnp.square(of - mu), axis=-1, keepdims=True)
    y = (of - mu) * lax.rsqrt(var + EPS) * g[None, :, None, :].astype(jnp.float32)
    return y.astype(o.dtype)


def memory_cross_attention(q, mem, mem_g, w_kv):
    mem_n = rmsnorm(mem, mem_g)
    kv = jnp.einsum('bmd,de->bme', mem_n, w_kv)
    km, vm = jnp.split(kv, 2, axis=-1)
    qh, kh, vh = split_heads(q, MEM_HEADS), split_heads(km, MEM_HEADS), split_heads(vm, MEM_HEADS)
    scores = jnp.einsum('bhsd,bhmd->bhsm', qh.astype(jnp.float32), kh.astype(jnp.float32)) * MEM_HD ** -0.5
    p = jax.nn.softmax(scores, axis=-1).astype(vh.dtype)
    return merge_heads(jnp.einsum('bhsm,bhmd->bhsd', p, vh))


def setup_inputs(seed: int = 0) -> dict:
    key = jax.random.key(seed)
    ks = jax.random.split(key, 24)
    f32 = jnp.float32

    def nrm(k, shape, scale):
        return jax.random.normal(k, shape, f32) * scale

    x = nrm(ks[0], (BATCH, SEQ, D_MODEL), 1.0)
    mem = nrm(ks[1], (BATCH, MEM_LEN, D_MODEL), 1.0)
    offset = jax.random.randint(ks[2], (BATCH, 1), 0, 4096, dtype=jnp.int32)
    positions = (offset + jnp.arange(SEQ, dtype=jnp.int32)[None, :]).astype(jnp.int32)

    norm_g = 1.0 + nrm(ks[3], (DEPTH, D_MODEL), 0.02)
    mem_norm_g = 1.0 + nrm(ks[4], (DEPTH, D_MODEL), 0.02)
    w_in = nrm(ks[5], (DEPTH, D_MODEL, IN_COLS), D_MODEL ** -0.5)
    b_gate = nrm(ks[6], (DEPTH, N_BRANCH * D_MODEL), 0.1)
    dn_conv_w = nrm(ks[7], (DEPTH, DN_CONV, 3 * DN_W), DN_CONV ** -0.5)
    dn_a_log = jnp.log(jax.random.uniform(ks[8], (DEPTH, DN_HEADS), f32, 1.0, 16.0))
    dt = jnp.exp(jax.random.uniform(ks[9], (DEPTH, DN_HEADS), f32, math.log(1e-3), math.log(1e-1)))
    dn_dt_bias = dt + jnp.log(-jnp.expm1(-dt))
    dn_norm_g = 1.0 + nrm(ks[10], (DEPTH, DN_HD), 0.02)
    ret_norm_g = 1.0 + nrm(ks[11], (DEPTH, RET_V_W), 0.02)
    w_mem_kv = nrm(ks[12], (DEPTH, D_MODEL, 2 * MEM_W), D_MODEL ** -0.5)
    w_br_sb = nrm(ks[13], (DEPTH, SB_W, D_MODEL), SB_W ** -0.5)
    w_br_dn = nrm(ks[14], (DEPTH, DN_W, D_MODEL), DN_W ** -0.5)
    w_br_ret = nrm(ks[15], (DEPTH, RET_V_W, D_MODEL), RET_V_W ** -0.5)
    w_br_mem = nrm(ks[16], (DEPTH, MEM_W, D_MODEL), MEM_W ** -0.5)
    w_out = nrm(ks[17], (DEPTH, D_MODEL, D_MODEL), D_MODEL ** -0.5)
    final_norm_g = 1.0 + nrm(ks[18], (D_MODEL,), 0.02)
    return {"x": x, "mem": mem, "positions": positions, "norm_g": norm_g,
            "mem_norm_g": mem_norm_g, "w_in": w_in, "b_gate": b_gate, "dn_conv_w": dn_conv_w,
            "dn_a_log": dn_a_log, "dn_dt_bias": dn_dt_bias, "dn_norm_g": dn_norm_g,
            "ret_norm_g": ret_norm_g, "w_mem_kv": w_mem_kv, "w_br_sb": w_br_sb,
            "w_br_dn": w_br_dn, "w_br_ret": w_br_ret, "w_br_mem": w_br_mem,
            "w_out": w_out, "final_norm_g": final_norm_g}


def reference(x, mem, positions, norm_g, mem_norm_g, w_in, b_gate, dn_conv_w, dn_a_log,
              dn_dt_bias, dn_norm_g, ret_norm_g, w_mem_kv, w_br_sb, w_br_dn, w_br_ret,
              w_br_mem, w_out, final_norm_g):
    b, s_len, _ = x.shape
    split_idx = np.cumsum(IN_SPLITS)[:-1].tolist()
    for l in range(DEPTH):
        h = rmsnorm(x, norm_g[l])
        proj = jnp.einsum('bsd,de->bse', h, w_in[l])
        (sb_q, sb_k, sb_v, sb_z,
         dn_q, dn_k, dn_v, dn_z, dn_a, dn_b,
         rt_q, rt_k, rt_v, rt_z,
         mem_q, gate_logits) = jnp.split(proj, split_idx, axis=-1)

        o_sb = stick_breaking_attention(split_heads(sb_q, SB_HEADS), split_heads(sb_k, SB_HEADS),
                                        split_heads(sb_v, SB_HEADS))
        o_sb = merge_heads(o_sb) * jax.nn.silu(sb_z)

        qkv = jax.nn.silu(causal_depthwise_conv(jnp.concatenate([dn_q, dn_k, dn_v], axis=-1), dn_conv_w[l]))
        cq, ck, cv = jnp.split(qkv, 3, axis=-1)
        cq = l2norm(split_heads(cq, DN_HEADS).astype(jnp.float32))
        ck = l2norm(split_heads(ck, DN_HEADS).astype(jnp.float32))
        cv = split_heads(cv, DN_HEADS)
        g_log = -jnp.exp(dn_a_log[l].astype(jnp.float32)) * jax.nn.softplus(
            dn_a.astype(jnp.float32) + dn_dt_bias[l].astype(jnp.float32))
        beta = jax.nn.sigmoid(dn_b.astype(jnp.float32))
        o_dn = gated_delta_rule(cq, ck, cv, g_log.transpose(0, 2, 1), beta.transpose(0, 2, 1))
        o_dn = merge_heads(rmsnorm(o_dn, dn_norm_g[l])) * jax.nn.silu(dn_z)

        rq = rope(split_heads(rt_q, RET_HEADS), positions)
        rk = rope(split_heads(rt_k, RET_HEADS), positions)
        o_rt = retention_chunkwise(rq, rk, split_heads(rt_v, RET_HEADS))
        o_rt = head_groupnorm(o_rt, ret_norm_g[l].reshape(RET_HEADS, RET_V_HD))
        o_rt = merge_heads(o_rt) * jax.nn.silu(rt_z)

        o_mem = memory_cross_attention(mem_q, mem, mem_norm_g[l], w_mem_kv[l])

        gates = jax.nn.sigmoid(gate_logits.astype(jnp.float32) + b_gate[l].astype(jnp.float32))
        gates = gates.reshape(b, s_len, N_BRANCH, D_MODEL).astype(x.dtype)
        merged = (gates[:, :, 0] * jnp.einsum('bsw,wd->bsd', o_sb, w_br_sb[l])
                  + gates[:, :, 1] * jnp.einsum('bsw,wd->bsd', o_dn, w_br_dn[l])
                  + gates[:, :, 2] * jnp.einsum('bsw,wd->bsd', o_rt, w_br_ret[l])
                  + gates[:, :, 3] * jnp.einsum('bsw,wd->bsd', o_mem, w_br_mem[l]))
        x = x + jnp.einsum('bsd,de->bse', merged, w_out[l])
    return rmsnorm(x, final_norm_g)
```

```python
import functools
import math

import jax
import jax.numpy as jnp
from jax import lax
from jax.experimental import pallas as pl
from jax.experimental.pallas import tpu as pltpu

F32 = jnp.float32
BF16 = jnp.bfloat16
EPS = 1e-6

D_MODEL = 1024
SB_HEADS, SB_HD = 8, 64
DN_HEADS, DN_HD, DN_CONV, DN_CHUNK = 4, 128, 4, 64
RET_HEADS, RET_QK_HD, RET_V_HD, RET_CHUNK = 4, 64, 128, 128
MEM_HEADS, MEM_HD = 4, 64
N_BRANCH = 4
ROPE_BASE = 10000.0

LANES = 128
VMEM_LIMIT = 56 << 20

COL_DN_QKV = 0
COL_DN_Z = 1536
COL_SB_Q = 2048
COL_SB_K = 2560
COL_SB_V = 3072
COL_SB_Z = 3584
COL_RT_V = 4096
COL_RT_Z = 4608
COL_RT_Q = 5120
COL_RT_K = 5376
COL_MEM_Q = 5632
COL_AB = 5888
PROJ_COLS = 6144

PROJ_TM, PROJ_TN = 1024, 512
SB_TQ, SB_TK = 256, 256
DN_TS = 512
RT_TS = 512
MEM_TS = 512
MERGE_TM = 512


def _params(*sem):
    return pltpu.CompilerParams(dimension_semantics=sem, vmem_limit_bytes=VMEM_LIMIT)


def _rms(x, g):
    return x * lax.rsqrt(jnp.mean(x * x, axis=-1, keepdims=True) + EPS) * g


def _silu(x):
    return x * jax.nn.sigmoid(x)


def _dot(a, b):
    return jnp.dot(a.astype(BF16), b.astype(BF16), preferred_element_type=F32)


def _dot_nt(a, b):
    return lax.dot_general(a.astype(BF16), b.astype(BF16), (((1,), (1,)), ((), ())),
                           preferred_element_type=F32)


def _dotf(a, b):
    return jnp.dot(a, b, preferred_element_type=F32, precision=lax.Precision.HIGHEST)


def _dotf_nt(a, b):
    return lax.dot_general(a, b, (((1,), (1,)), ((), ())), preferred_element_type=F32,
                           precision=lax.Precision.HIGHEST)


def _proj_kernel(x_ref, g_ref, w_ref, o_ref, h_ref):
    @pl.when(pl.program_id(1) == 0)
    def _():
        h_ref[...] = _rms(x_ref[...], g_ref[...]).astype(BF16)

    o_ref[...] = jnp.dot(h_ref[...], w_ref[...], preferred_element_type=F32)


def _proj_call(xf, g, w_a):
    m, d = xf.shape
    n = w_a.shape[1]
    return pl.pallas_call(
        _proj_kernel,
        grid=(m // PROJ_TM, n // PROJ_TN),
        in_specs=[pl.BlockSpec((PROJ_TM, d), lambda i, j: (i, 0)),
                  pl.BlockSpec((1, d), lambda i, j: (0, 0)),
                  pl.BlockSpec((d, PROJ_TN), lambda i, j: (0, j))],
        out_specs=pl.BlockSpec((PROJ_TM, PROJ_TN), lambda i, j: (i, j)),
        out_shape=jax.ShapeDtypeStruct((m, n), F32),
        scratch_shapes=[pltpu.VMEM((PROJ_TM, d), BF16)],
        compiler_params=_params("arbitrary", "arbitrary"),
        name="proj",
    )(xf, g, w_a)


def _sb_kernel(q_ref, k_ref, v_ref, z_ref, o_ref, *, tq, tk):
    q0 = pl.multiple_of(pl.program_id(2) * tq, tq)
    lane = lax.broadcasted_iota(jnp.int32, (1, LANES), 1)
    row = lax.broadcasted_iota(jnp.int32, (tk, tk), 0)
    col = lax.broadcasted_iota(jnp.int32, (tk, tk), 1)
    tri = (row >= col).astype(BF16)
    t_idx = q0 + lax.broadcasted_iota(jnp.int32, (tq, tk), 0)
    s_loc = lax.broadcasted_iota(jnp.int32, (tq, tk), 1)
    q = q_ref[...] * (SB_HD ** -0.5)
    n_diag = tq // tk
    n_full = q0 // tk

    def block(qm, k0, carry, acc, masked):
        kb = k_ref[pl.ds(k0, tk), :]
        vb = v_ref[pl.ds(k0, tk), :]
        z = _dot_nt(qm, kb)
        ls = jnp.minimum(-z, 0.0) - jnp.log(1.0 + jnp.exp(-jnp.abs(z)))
        if masked:
            causal = (k0 + s_loc) < t_idx
            ls = jnp.where(causal, ls, 0.0)
        hi = ls.astype(BF16)
        lo = (ls - hi.astype(F32)).astype(BF16)
        incl = (jnp.dot(hi, tri, preferred_element_type=F32)
                + jnp.dot(lo, tri, preferred_element_type=F32))
        w = jnp.exp(z + incl + carry)
        if masked:
            w = jnp.where(causal, w, 0.0)
        acc = acc + _dot(w, vb)
        carry = carry + incl[:, 0:1]
        return carry, acc

    outs = []
    for hh in range(2):
        in_head = (lane >= hh * SB_HD) & (lane < (hh + 1) * SB_HD)
        qm = jnp.where(in_head, q, 0.0).astype(BF16)
        carry = jnp.zeros((tq, 1), F32)
        acc = jnp.zeros((tq, LANES), F32)
        for d in range(n_diag - 1, -1, -1):
            k0 = pl.multiple_of(q0 + d * tk, tk)
            carry, acc = block(qm, k0, carry, acc, True)

        def body(it, c, qm=qm):
            k0 = pl.multiple_of((n_full - 1 - it) * tk, tk)
            return block(qm, k0, c[0], c[1], False)

        carry, acc = lax.fori_loop(0, n_full, body, (carry, acc))
        outs.append(acc)
    o = jnp.where(lane < SB_HD, outs[0], outs[1])
    o_ref[...] = o * _silu(z_ref[...])


def _sb_call(proj, batch, seq):
    m = proj.shape[0]
    tq, tk = SB_TQ, SB_TK
    nq = seq // tq
    pairs = SB_HEADS * SB_HD // LANES
    cq, ck, cv, cz = (c // LANES for c in (COL_SB_Q, COL_SB_K, COL_SB_V, COL_SB_Z))
    return pl.pallas_call(
        functools.partial(_sb_kernel, tq=tq, tk=tk),
        grid=(batch, pairs, nq),
        in_specs=[pl.BlockSpec((tq, LANES), lambda b, p, i: (b * nq + i, cq + p)),
                  pl.BlockSpec((seq, LANES), lambda b, p, i: (b, ck + p)),
                  pl.BlockSpec((seq, LANES), lambda b, p, i: (b, cv + p)),
                  pl.BlockSpec((tq, LANES), lambda b, p, i: (b * nq + i, cz + p))],
        out_specs=pl.BlockSpec((tq, LANES), lambda b, p, i: (b * nq + i, p)),
        out_shape=jax.ShapeDtypeStruct((m, SB_HEADS * SB_HD), F32),
        compiler_params=_params("arbitrary", "arbitrary", "arbitrary"),
        name="sb_attn",
    )(proj, proj, proj, proj)


def _unit_lower_inverse(low):
    c = low.shape[0]
    eye = (lax.broadcasted_iota(jnp.int32, (c, c), 0)
           == lax.broadcasted_iota(jnp.int32, (c, c), 1)).astype(F32)
    inv = eye - low
    power = low
    for _ in range(int(math.log2(c)) - 1):
        power = _dotf(power, power)
        inv = inv + _dotf(inv, power)
    return inv


def _dn_kernel(qkv_ref, z_ref, ab_ref, cw_ref, alog_ref, dt_ref, ng_ref, o_ref,
               ext_ref, xc_ref, state_ref, *, ts):
    c = DN_CHUNK
    w_dn = DN_HEADS * DN_HD
    pad = 8

    @pl.when(pl.program_id(1) == 0)
    def _():
        ext_ref[0:pad, :] = jnp.zeros((pad, 3 * w_dn), F32)
        state_ref[...] = jnp.zeros_like(state_ref)

    @pl.when(pl.program_id(1) > 0)
    def _():
        ext_ref[0:pad, :] = ext_ref[ts:ts + pad, :]

    ext_ref[pad:pad + ts, :] = qkv_ref[...]

    ri = lax.broadcasted_iota(jnp.int32, (c, c), 0)
    ci = lax.broadcasted_iota(jnp.int32, (c, c), 1)
    tril = ri >= ci
    strict = ri > ci
    tril_f = tril.astype(F32)

    for col0 in range(0, 3 * w_dn, LANES):
        acc = None
        for j in range(DN_CONV):
            lo = pad - (DN_CONV - 1) + j
            term = ext_ref[lo:lo + ts, col0:col0 + LANES] * cw_ref[j:j + 1, col0:col0 + LANES]
            acc = term if acc is None else acc + term
        xc_ref[:, col0:col0 + LANES] = _silu(acc)

    def conv_silu(r0, col0):
        return xc_ref[pl.ds(r0, c), col0:col0 + DN_HD]

    def l2n(t):
        return t * lax.rsqrt(jnp.sum(t * t, axis=-1, keepdims=True) + EPS)

    def chunk_body(ic, _):
        r0 = pl.multiple_of(ic * c, c)
        ab = ab_ref[pl.ds(r0, c), :]
        sp = jnp.maximum(ab + dt_ref[...], 0.0) + jnp.log(
            1.0 + jnp.exp(-jnp.abs(ab + dt_ref[...])))
        g_all = -jnp.exp(alog_ref[...]) * sp
        beta_all = jax.nn.sigmoid(ab)
        gc_all = _dotf(tril_f, g_all)
        gc_t = gc_all.T
        for h in range(DN_HEADS):
            qh = l2n(conv_silu(r0, h * DN_HD)) * (DN_HD ** -0.5)
            kh = l2n(conv_silu(r0, w_dn + h * DN_HD))
            vh = conv_silu(r0, 2 * w_dn + h * DN_HD)
            gcol = gc_all[:, h:h + 1]
            grow = gc_t[h:h + 1, :]
            beta = beta_all[:, DN_HEADS + h:DN_HEADS + h + 1]
            decay = jnp.where(tril, jnp.exp(jnp.minimum(gcol - grow, 0.0)), 0.0)
            k_beta = kh * beta
            v_beta = vh * beta
            low = jnp.where(strict, _dotf_nt(k_beta, kh) * decay, 0.0)
            t_inv = _unit_lower_inverse(low)
            egc = jnp.exp(gcol)
            u = _dotf(t_inv, v_beta)
            w = _dotf(t_inv, k_beta * egc)
            a_intra = jnp.where(tril, _dotf_nt(qh, kh) * decay, 0.0)
            state = state_ref[h]
            v_new = u - _dotf(w, state)
            o = _dotf(qh * egc, state) + _dotf(a_intra, v_new)
            g_last = gc_all[c - 1:c, h:h + 1]
            k_dec = kh * jnp.exp(g_last - gcol)
            state_ref[h] = state * jnp.exp(g_last) + _dotf(k_dec.T, v_new)
            zh = z_ref[pl.ds(r0, c), h * DN_HD:(h + 1) * DN_HD]
            o_ref[pl.ds(r0, c), h * DN_HD:(h + 1) * DN_HD] = _rms(o, ng_ref[...]) * _silu(zh)
        return 0

    lax.fori_loop(0, ts // c, chunk_body, 0)


def _dn_call(proj, conv_w, a_log_pad, dt_pad, norm_g, batch, seq):
    m = proj.shape[0]
    ts = DN_TS
    ns = seq // ts
    w_dn = DN_HEADS * DN_HD
    return pl.pallas_call(
        functools.partial(_dn_kernel, ts=ts),
        grid=(batch, ns),
        in_specs=[pl.BlockSpec((ts, 3 * w_dn), lambda b, s: (b * ns + s, COL_DN_QKV // (3 * w_dn))),
                  pl.BlockSpec((ts, w_dn), lambda b, s: (b * ns + s, COL_DN_Z // w_dn)),
                  pl.BlockSpec((ts, LANES), lambda b, s: (b * ns + s, COL_AB // LANES)),
                  pl.BlockSpec((DN_CONV, 3 * w_dn), lambda b, s: (0, 0)),
                  pl.BlockSpec((1, LANES), lambda b, s: (0, 0)),
                  pl.BlockSpec((1, LANES), lambda b, s: (0, 0)),
                  pl.BlockSpec((1, DN_HD), lambda b, s: (0, 0))],
        out_specs=pl.BlockSpec((ts, w_dn), lambda b, s: (b * ns + s, 0)),
        out_shape=jax.ShapeDtypeStruct((m, w_dn), F32),
        scratch_shapes=[pltpu.VMEM((ts + 8, 3 * w_dn), F32),
                        pltpu.VMEM((ts, 3 * w_dn), F32),
                        pltpu.VMEM((DN_HEADS, DN_HD, DN_HD), F32)],
        compiler_params=_params("arbitrary", "arbitrary"),
        name="deltanet",
    )(proj, proj, proj, conv_w, a_log_pad, dt_pad, norm_g)


def _ret_kernel(q_ref, k_ref, v_ref, z_ref, pos_ref, inv_ref, g_ref, o_ref, state_ref, *, ts):
    c = RET_CHUNK
    w_qk = RET_HEADS * RET_QK_HD
    half = RET_QK_HD // 2

    @pl.when(pl.program_id(1) == 0)
    def _():
        state_ref[...] = jnp.zeros_like(state_ref)

    lane = lax.broadcasted_iota(jnp.int32, (1, w_qk), 1)
    first_half = (lane % RET_QK_HD) < half
    ang = pos_ref[0] * inv_ref[...]
    cos, sin = jnp.cos(ang), jnp.sin(ang)

    def rope(t):
        upper = pltpu.roll(t, w_qk - half, 1)
        lower = pltpu.roll(t, half, 1)
        return t * cos + jnp.where(first_half, -upper, lower) * sin

    q = rope(q_ref[...]) * (RET_QK_HD ** -0.5)
    k = rope(k_ref[...])

    ri = lax.broadcasted_iota(jnp.int32, (c, c), 0)
    ci = lax.broadcasted_iota(jnp.int32, (c, c), 1)
    rel = (ri - ci).astype(F32)
    idx = lax.broadcasted_iota(jnp.int32, (c, 1), 0).astype(F32)

    for h in range(RET_HEADS):
        log_gamma = math.log1p(-(2.0 ** (-5.0 - h)))
        intra = jnp.where(rel >= 0.0, jnp.exp(jnp.maximum(rel, 0.0) * log_gamma), 0.0)
        cross = jnp.exp((idx + 1.0) * log_gamma)
        sdec = jnp.exp((c - 1.0 - idx) * log_gamma)
        chunk_decay = math.exp(c * log_gamma)
        in_head = (lane >= h * RET_QK_HD) & (lane < (h + 1) * RET_QK_HD)
        state = state_ref[h]
        for ic in range(ts // c):
            rows = slice(ic * c, (ic + 1) * c)
            qh = jnp.where(in_head, q[rows], 0.0)
            kc = k[rows]
            vh = v_ref[rows, h * RET_V_HD:(h + 1) * RET_V_HD]
            scores = _dot_nt(qh, kc) * intra
            o = _dot(scores, vh) + _dot(qh, state) * cross
            kd = jnp.where(in_head, kc, 0.0) * sdec
            state = state * chunk_decay + _dot(kd.T, vh)
            mu = jnp.mean(o, axis=-1, keepdims=True)
            var = jnp.mean(jnp.square(o - mu), axis=-1, keepdims=True)
            y = (o - mu) * lax.rsqrt(var + EPS) * g_ref[:, h * RET_V_HD:(h + 1) * RET_V_HD]
            zh = z_ref[rows, h * RET_V_HD:(h + 1) * RET_V_HD]
            o_ref[rows, h * RET_V_HD:(h + 1) * RET_V_HD] = y * _silu(zh)
        state_ref[h] = state


def _ret_call(proj, pos_f, inv_lane, norm_g, batch, seq):
    m = proj.shape[0]
    ts = RT_TS
    ns = seq // ts
    w_qk = RET_HEADS * RET_QK_HD
    w_v = RET_HEADS * RET_V_HD
    return pl.pallas_call(
        functools.partial(_ret_kernel, ts=ts),
        grid=(batch, ns),
        in_specs=[pl.BlockSpec((ts, w_qk), lambda b, s: (b * ns + s, COL_RT_Q // w_qk)),
                  pl.BlockSpec((ts, w_qk), lambda b, s: (b * ns + s, COL_RT_K // w_qk)),
                  pl.BlockSpec((ts, w_v), lambda b, s: (b * ns + s, COL_RT_V // w_v)),
                  pl.BlockSpec((ts, w_v), lambda b, s: (b * ns + s, COL_RT_Z // w_v)),
                  pl.BlockSpec((1, ts, 1), lambda b, s: (b, s, 0)),
                  pl.BlockSpec((1, w_qk), lambda b, s: (0, 0)),
                  pl.BlockSpec((1, w_v), lambda b, s: (0, 0))],
        out_specs=pl.BlockSpec((ts, w_v), lambda b, s: (b * ns + s, 0)),
        out_shape=jax.ShapeDtypeStruct((m, w_v), F32),
        scratch_shapes=[pltpu.VMEM((RET_HEADS, w_qk, RET_V_HD), F32)],
        compiler_params=_params("arbitrary", "arbitrary"),
        name="retention",
    )(proj, proj, proj, proj, pos_f, inv_lane, norm_g)


def _mem_kernel(q_ref, mem_ref, mg_ref, wkv_ref, o_ref, kv_ref):
    w_mem = MEM_HEADS * MEM_HD

    @pl.when(pl.program_id(1) == 0)
    def _():
        mem_n = _rms(mem_ref[0], mg_ref[...])
        kv_ref[...] = _dot(mem_n, wkv_ref[...])

    lane = lax.broadcasted_iota(jnp.int32, (1, w_mem), 1)
    q = q_ref[...] * (MEM_HD ** -0.5)
    km = kv_ref[:, 0:w_mem]
    vm = kv_ref[:, w_mem:2 * w_mem]
    out = jnp.zeros(q.shape, F32)
    for h in range(MEM_HEADS):
        in_head = (lane >= h * MEM_HD) & (lane < (h + 1) * MEM_HD)
        scores = _dot_nt(jnp.where(in_head, q, 0.0), km)
        e = jnp.exp(scores - jnp.max(scores, axis=-1, keepdims=True))
        p = e / jnp.sum(e, axis=-1, keepdims=True)
        out = jnp.where(in_head, _dot(p, vm), out)
    o_ref[...] = out


def _mem_call(proj, mem, mem_g, w_kv, batch, seq):
    m = proj.shape[0]
    ts = MEM_TS
    ns = seq // ts
    w_mem = MEM_HEADS * MEM_HD
    mem_len, d = mem.shape[1], mem.shape[2]
    return pl.pallas_call(
        _mem_kernel,
        grid=(batch, ns),
        in_specs=[pl.BlockSpec((ts, w_mem), lambda b, s: (b * ns + s, COL_MEM_Q // w_mem)),
                  pl.BlockSpec((1, mem_len, d), lambda b, s: (b, 0, 0)),
                  pl.BlockSpec((1, d), lambda b, s: (0, 0)),
                  pl.BlockSpec((d, 2 * w_mem), lambda b, s: (0, 0))],
        out_specs=pl.BlockSpec((ts, w_mem), lambda b, s: (b * ns + s, 0)),
        out_shape=jax.ShapeDtypeStruct((m, w_mem), F32),
        scratch_shapes=[pltpu.VMEM((mem_len, 2 * w_mem), F32)],
        compiler_params=_params("arbitrary", "arbitrary"),
        name="mem_attn",
    )(proj, mem, mem_g, w_kv)


def _merge_kernel(x_ref, osb_ref, odn_ref, ort_ref, omem_ref, g_ref, wg_ref, bg_ref,
                  wsb_ref, wdn_ref, wrt_ref, wmem_ref, wout_ref, fg_ref, o_ref, *, final):
    d = x_ref.shape[1]
    x = x_ref[...]
    h = _rms(x, g_ref[...]).astype(BF16)
    merged = None
    branches = ((osb_ref, wsb_ref), (odn_ref, wdn_ref), (ort_ref, wrt_ref), (omem_ref, wmem_ref))
    for br, (ob_ref, wb_ref) in enumerate(branches):
        logits = jnp.dot(h, wg_ref[:, br * d:(br + 1) * d], preferred_element_type=F32)
        gate = jax.nn.sigmoid(logits + bg_ref[:, br * d:(br + 1) * d])
        term = gate * _dot(ob_ref[...], wb_ref[...])
        merged = term if merged is None else merged + term
    y = x + _dot(merged, wout_ref[...])
    if final:
        y = _rms(y, fg_ref[...])
    o_ref[...] = y


def _merge_call(xf, o_sb, o_dn, o_rt, o_mem, g, w_g, b_g, w_sb, w_dn, w_rt, w_mem, w_out, fg, final):
    m, d = xf.shape
    tm = MERGE_TM

    def rows(width):
        return pl.BlockSpec((tm, width), lambda i: (i, 0))

    def whole(a):
        return pl.BlockSpec(a.shape, lambda i: (0, 0), pipeline_mode=pl.Buffered(1))

    args = (xf, o_sb, o_dn, o_rt, o_mem, g, w_g, b_g, w_sb, w_dn, w_rt, w_mem, w_out, fg)
    in_specs = [rows(a.shape[1]) for a in args[:5]] + [whole(a) for a in args[5:]]
    return pl.pallas_call(
        functools.partial(_merge_kernel, final=final),
        grid=(m // tm,),
        in_specs=in_specs,
        out_specs=rows(d),
        out_shape=jax.ShapeDtypeStruct((m, d), F32),
        compiler_params=_params("arbitrary"),
        name="merge",
    )(*args)


def _arrange_w_in(w):
    sb = SB_HEADS * SB_HD
    dn = DN_HEADS * DN_HD
    o = 0
    sb_qkvz = w[:, o:o + 4 * sb]; o += 4 * sb
    dn_qkv = w[:, o:o + 3 * dn]; o += 3 * dn
    dn_z = w[:, o:o + dn]; o += dn
    ab = w[:, o:o + 2 * DN_HEADS]; o += 2 * DN_HEADS
    rt_q = w[:, o:o + 256]; o += 256
    rt_k = w[:, o:o + 256]; o += 256
    rt_v = w[:, o:o + 512]; o += 512
    rt_z = w[:, o:o + 512]; o += 512
    mem_q = w[:, o:o + 256]; o += 256
    gates = w[:, o:]
    ab_pad = jnp.pad(ab, ((0, 0), (0, LANES - 2 * DN_HEADS)))
    tail = jnp.zeros((w.shape[0], PROJ_COLS - COL_AB - LANES), w.dtype)
    w_a = jnp.concatenate([dn_qkv, dn_z, sb_qkvz, rt_v, rt_z, rt_q, rt_k, mem_q, ab_pad, tail], axis=1)
    return w_a.astype(BF16), gates.astype(BF16)


def kernel(x, mem, positions, norm_g, mem_norm_g, w_in, b_gate, dn_conv_w, dn_a_log, dn_dt_bias,
           dn_norm_g, ret_norm_g, w_mem_kv, w_br_sb, w_br_dn, w_br_ret, w_br_mem, w_out, final_norm_g):
    batch, seq, d = x.shape
    depth = w_in.shape[0]
    xf = x.reshape(batch * seq, d)
    pos_f = positions.astype(F32)[..., None]
    half = RET_QK_HD // 2
    inv = ROPE_BASE ** (-jnp.arange(half, dtype=F32) / half)
    inv_lane = jnp.tile(inv, RET_HEADS * 2)[None, :]
    lane_pad = LANES - DN_HEADS
    for l in range(depth):
        w_a, w_g = _arrange_w_in(w_in[l])
        proj = _proj_call(xf, norm_g[l][None, :], w_a)
        o_sb = _sb_call(proj, batch, seq)
        o_dn = _dn_call(proj, dn_conv_w[l],
                        jnp.pad(dn_a_log[l], (0, lane_pad))[None, :],
                        jnp.pad(dn_dt_bias[l], (0, lane_pad))[None, :],
                        dn_norm_g[l][None, :], batch, seq)
        o_rt = _ret_call(proj, pos_f, inv_lane, ret_norm_g[l][None, :], batch, seq)
        o_mem = _mem_call(proj, mem, mem_norm_g[l][None, :], w_mem_kv[l].astype(BF16), batch, seq)
        xf = _merge_call(xf, o_sb, o_dn, o_rt, o_mem, norm_g[l][None, :], w_g, b_gate[l][None, :],
                         w_br_sb[l].astype(BF16), w_br_dn[l].astype(BF16), w_br_ret[l].astype(BF16),
                         w_br_mem[l].astype(BF16), w_out[l].astype(BF16), final_norm_g[None, :],
                         final=(l == depth - 1))
    return xf.reshape(batch, seq, d)
```

```python
import functools
import math

import jax
import jax.numpy as jnp
from jax import lax
from jax.experimental import pallas as pl
from jax.experimental.pallas import tpu as pltpu

F32 = jnp.float32
BF16 = jnp.bfloat16
EPS = 1e-6

D_MODEL = 1024
SB_HEADS, SB_HD = 8, 64
DN_HEADS, DN_HD, DN_CONV, DN_CHUNK = 4, 128, 4, 128
RET_HEADS, RET_QK_HD, RET_V_HD, RET_CHUNK = 4, 64, 128, 128
MEM_HEADS, MEM_HD = 4, 64
N_BRANCH = 4
ROPE_BASE = 10000.0
LOG2E = 1.4426950408889634

LANES = 128
VMEM_LIMIT = 56 << 20

COL_DN_QKV = 0
COL_DN_Z = 1536
COL_SB_Q = 2048
COL_SB_K = 2560
COL_SB_V = 3072
COL_SB_Z = 3584
COL_RT_V = 4096
COL_RT_Z = 4608
COL_RT_Q = 5120
COL_RT_K = 5376
COL_MEM_Q = 5632
COL_AB = 5888
PROJ_COLS = 6144

PROJ_TM, PROJ_TN = 1024, 512
SB_TQ, SB_TK = 512, 256
DN_TS = 512
RT_TS = 512
MEM_TS = 512
MERGE_TM = 512

PROJ_DTYPE = BF16
BRANCH_DTYPE = BF16


def _params(*sem):
    return pltpu.CompilerParams(dimension_semantics=sem, vmem_limit_bytes=VMEM_LIMIT)


def _rms(x, g):
    return x * lax.rsqrt(jnp.mean(x * x, axis=-1, keepdims=True) + EPS) * g


def _silu(x):
    half = 0.5 * x
    return half + half * jnp.tanh(half)


def _dot(a, b):
    return jnp.dot(a.astype(BF16), b.astype(BF16), preferred_element_type=F32)


def _dot_nt(a, b):
    return lax.dot_general(a.astype(BF16), b.astype(BF16), (((1,), (1,)), ((), ())),
                           preferred_element_type=F32)


def _proj_kernel(x_ref, g_ref, w_ref, o_ref):
    h = _rms(x_ref[...], g_ref[...]).astype(BF16)
    n = w_ref.shape[1]
    for c0 in range(0, n, PROJ_TN):
        o_ref[:, c0:c0 + PROJ_TN] = jnp.dot(h, w_ref[:, c0:c0 + PROJ_TN],
                                            preferred_element_type=F32).astype(o_ref.dtype)


def _proj_call(xf, g, w_a):
    m, d = xf.shape
    n = w_a.shape[1]
    return pl.pallas_call(
        _proj_kernel,
        grid=(m // PROJ_TM,),
        in_specs=[pl.BlockSpec((PROJ_TM, d), lambda i: (i, 0)),
                  pl.BlockSpec((1, d), lambda i: (0, 0), pipeline_mode=pl.Buffered(1)),
                  pl.BlockSpec((d, n), lambda i: (0, 0), pipeline_mode=pl.Buffered(1))],
        out_specs=pl.BlockSpec((PROJ_TM, n), lambda i: (i, 0)),
        out_shape=jax.ShapeDtypeStruct((m, n), PROJ_DTYPE),
        compiler_params=_params("arbitrary"),
        name="proj",
    )(xf, g, w_a)


def _sb_kernel(q_ref, k_ref, v_ref, z_ref, o_ref, *, tq, tk):
    halves = tq // tk
    q0 = pl.multiple_of(pl.program_id(2) * tq, tq)
    lane = lax.broadcasted_iota(jnp.int32, (1, LANES), 1)
    row = lax.broadcasted_iota(jnp.int32, (tk, tk), 0)
    col = lax.broadcasted_iota(jnp.int32, (tk, tk), 1)
    tri = (row >= col).astype(BF16)
    diag_mask = col < row
    head_lanes = [(lane >= hh * SB_HD) & (lane < (hh + 1) * SB_HD) for hh in range(2)]

    q = q_ref[...].astype(F32) * (SB_HD ** -0.5 * LOG2E)
    qm = [[jnp.where(head_lanes[hh], q[r * tk:(r + 1) * tk], 0.0).astype(BF16) for hh in range(2)]
          for r in range(halves)]

    def step(k0, state, active):
        kb = k_ref[pl.ds(k0, tk), :].astype(BF16)
        vb = v_ref[pl.ds(k0, tk), :]
        vh = [jnp.where(head_lanes[hh], vb, jnp.zeros_like(vb)).astype(BF16) for hh in range(2)]
        chains = [(r, hh, mask) for r, mask in active.items() for hh in range(2)]
        z2s = [_dot_nt(qm[r][hh], kb) for r, hh, _ in chains]
        sp2s = []
        for z2, (_, _, mask) in zip(z2s, chains):
            sp2 = jnp.maximum(z2, 0.0) + jnp.log2(1.0 + jnp.exp2(-jnp.abs(z2)))
            sp2s.append(sp2 if mask is None else jnp.where(mask, sp2, 0.0))
        incls = [jnp.dot(sp2.astype(BF16), tri, preferred_element_type=F32) for sp2 in sp2s]
        ws = []
        for z2, incl, (r, hh, mask) in zip(z2s, incls, chains):
            w = jnp.exp2(z2 - incl - state[r][hh])
            ws.append((w if mask is None else jnp.where(mask, w, 0.0)).astype(BF16))
        out = list(state)
        for r in active:
            idx = [i for i, ch in enumerate(chains) if ch[0] == r]
            acc = state[r][2]
            carries = []
            for i in idx:
                hh = chains[i][1]
                acc = acc + jnp.dot(ws[i], vh[hh], preferred_element_type=F32)
                carries.append(state[r][hh] + incls[i][:, 0:1])
            out[r] = (carries[0], carries[1], acc)
        return tuple(out)

    zero_c = jnp.zeros((tk, 1), F32)
    state = tuple((zero_c, zero_c, jnp.zeros((tk, LANES), F32)) for _ in range(halves))
    for d in range(halves - 1, -1, -1):
        active = {r: (diag_mask if r == d else None) for r in range(d, halves)}
        state = step(pl.multiple_of(q0 + d * tk, tk), state, active)

    n_full = q0 // tk
    all_rows = {r: None for r in range(halves)}

    def body(it, st):
        return step(pl.multiple_of((n_full - 1 - it) * tk, tk), st, all_rows)

    state = lax.fori_loop(0, n_full, body, state)
    for r in range(halves):
        rows = slice(r * tk, (r + 1) * tk)
        o_ref[rows, :] = (state[r][2] * _silu(z_ref[rows, :].astype(F32))).astype(o_ref.dtype)


def _sb_call(proj, batch, seq):
    m = proj.shape[0]
    tq, tk = SB_TQ, SB_TK
    nq = seq // tq
    pairs = SB_HEADS * SB_HD // LANES
    cq, ck, cv, cz = (c // LANES for c in (COL_SB_Q, COL_SB_K, COL_SB_V, COL_SB_Z))
    return pl.pallas_call(
        functools.partial(_sb_kernel, tq=tq, tk=tk),
        grid=(batch, pairs, nq),
        in_specs=[pl.BlockSpec((tq, LANES), lambda b, p, i: (b * nq + i, cq + p)),
                  pl.BlockSpec((seq, LANES), lambda b, p, i: (b, ck + p)),
                  pl.BlockSpec((seq, LANES), lambda b, p, i: (b, cv + p)),
                  pl.BlockSpec((tq, LANES), lambda b, p, i: (b * nq + i, cz + p))],
        out_specs=pl.BlockSpec((tq, LANES), lambda b, p, i: (b * nq + i, p)),
        out_shape=jax.ShapeDtypeStruct((m, SB_HEADS * SB_HD), BRANCH_DTYPE),
        compiler_params=_params("arbitrary", "arbitrary", "arbitrary"),
        name="sb_attn",
    )(proj, proj, proj, proj)


def _level_masks(c):
    ri = lax.broadcasted_iota(jnp.int32, (c, c), 0)
    ci = lax.broadcasted_iota(jnp.int32, (c, c), 1)
    masks = [((ri == ci + 1) & ((ri & 1) == 1)).astype(F32)]
    b = 2
    while b < c:
        shift = int(math.log2(2 * b))
        masks.append((((ri >> shift) == (ci >> shift)) & ((ri & b) != 0) & ((ci & b) == 0)).astype(F32))
        b *= 2
    return masks


def _unit_lower_inverses(lows, masks, eye):
    invs = [eye - low * masks[0] for low in lows]
    for m in masks[1:]:
        inner = [_dot(low * m, inv) for low, inv in zip(lows, invs)]
        invs = [inv - _dot(inv, x) for inv, x in zip(invs, inner)]
    return invs


def _dn_kernel(qkv_ref, z_ref, ab_ref, cw_ref, alog_ref, dt_ref, ng_ref, o_ref,
               ext_ref, xc_ref, state_ref, *, ts):
    c = DN_CHUNK
    w_dn = DN_HEADS * DN_HD
    pad = 8

    @pl.when(pl.program_id(1) == 0)
    def _():
        ext_ref[0:pad, :] = jnp.zeros((pad, 3 * w_dn), F32)
        state_ref[...] = jnp.zeros_like(state_ref)

    @pl.when(pl.program_id(1) > 0)
    def _():
        ext_ref[0:pad, :] = ext_ref[ts:ts + pad, :]

    ext_ref[pad:pad + ts, :] = qkv_ref[...].astype(F32)

    for col0 in range(0, 3 * w_dn, LANES):
        acc = None
        for j in range(DN_CONV):
            lo = pad - (DN_CONV - 1) + j
            term = ext_ref[lo:lo + ts, col0:col0 + LANES] * cw_ref[j:j + 1, col0:col0 + LANES]
            acc = term if acc is None else acc + term
        xc_ref[:, col0:col0 + LANES] = _silu(acc)

    ri = lax.broadcasted_iota(jnp.int32, (c, c), 0)
    ci = lax.broadcasted_iota(jnp.int32, (c, c), 1)
    tril = ri >= ci
    strict = ri > ci
    tril_b = tril.astype(BF16)

    def l2n(t):
        return t * lax.rsqrt(jnp.sum(t * t, axis=-1, keepdims=True) + EPS)

    masks = _level_masks(c)
    eye = (ri == ci).astype(F32)
    heads = range(DN_HEADS)

    def local_part(ic):
        rows = slice(ic * c, (ic + 1) * c)
        ab = ab_ref[rows, :].astype(F32)
        pre = ab + dt_ref[...]
        sp = jnp.maximum(pre, 0.0) + jnp.log(1.0 + jnp.exp(-jnp.abs(pre)))
        g_all = -jnp.exp(alog_ref[...]) * sp
        beta_all = jax.nn.sigmoid(ab)
        g_hi = g_all.astype(BF16)
        g_lo = (g_all - g_hi.astype(F32)).astype(BF16)
        gc_all = (jnp.dot(tril_b, g_hi, preferred_element_type=F32)
                  + jnp.dot(tril_b, g_lo, preferred_element_type=F32))
        gc_t = gc_all.T
        qs = [l2n(xc_ref[rows, h * DN_HD:(h + 1) * DN_HD]) * (DN_HD ** -0.5) for h in heads]
        ks = [l2n(xc_ref[rows, w_dn + h * DN_HD:w_dn + (h + 1) * DN_HD]) for h in heads]
        vs = [xc_ref[rows, 2 * w_dn + h * DN_HD:2 * w_dn + (h + 1) * DN_HD] for h in heads]
        gcols = [gc_all[:, h:h + 1] for h in heads]
        betas = [beta_all[:, DN_HEADS + h:DN_HEADS + h + 1] for h in heads]
        decays = [jnp.where(tril, jnp.exp(jnp.minimum(gcols[h] - gc_t[h:h + 1, :], 0.0)), 0.0) for h in heads]
        k_betas = [ks[h] * betas[h] for h in heads]
        kqs = [_dot_nt(jnp.concatenate([k_betas[h], qs[h]], axis=0), ks[h]) for h in heads]
        lows = [jnp.where(strict, kqs[h][:c] * decays[h], 0.0) for h in heads]
        a_intras = [jnp.where(tril, kqs[h][c:] * decays[h], 0.0) for h in heads]
        egcs = [jnp.exp(gcols[h]) for h in heads]
        rhs = [jnp.concatenate([vs[h] * betas[h], k_betas[h] * egcs[h]], axis=1) for h in heads]
        q_egcs = [qs[h] * egcs[h] for h in heads]
        g_lasts = [gc_all[c - 1:c, h:h + 1] for h in heads]
        k_decs = [ks[h] * jnp.exp(g_lasts[h] - gcols[h]) for h in heads]
        e_lasts = [jnp.exp(g_lasts[h]) for h in heads]
        return lows, rhs, q_egcs, a_intras, k_decs, e_lasts

    n_chunks = ts // c
    parts = [local_part(ic) for ic in range(n_chunks)]
    t_invs = _unit_lower_inverses([low for p in parts for low in p[0]], masks, eye)
    uws = [_dot(t, r) for t, r in zip(t_invs, [r for p in parts for r in p[1]])]

    states = [state_ref[h] for h in heads]
    for ic in range(n_chunks):
        rows = slice(ic * c, (ic + 1) * c)
        _, _, q_egcs, a_intras, k_decs, e_lasts = parts[ic]
        uw = uws[ic * DN_HEADS:(ic + 1) * DN_HEADS]
        wss = [_dot(jnp.concatenate([uw[h][:, DN_HD:], q_egcs[h]], axis=0), states[h])
               for h in heads]
        v_news = [uw[h][:, :DN_HD] - wss[h][:c] for h in heads]
        states = [states[h] * e_lasts[h] + lax.dot_general(
            k_decs[h].astype(BF16), v_news[h].astype(BF16), (((0,), (0,)), ((), ())),
            preferred_element_type=F32) for h in heads]
        outs = [wss[h][c:] + _dot(a_intras[h], v_news[h]) for h in heads]
        for h in heads:
            cols = slice(h * DN_HD, (h + 1) * DN_HD)
            o_ref[rows, cols] = (_rms(outs[h], ng_ref[...])
                                 * _silu(z_ref[rows, cols].astype(F32))).astype(o_ref.dtype)
    for h in heads:
        state_ref[h] = states[h]


def _dn_call(proj, conv_w, a_log_pad, dt_pad, norm_g, batch, seq):
    m = proj.shape[0]
    ts = DN_TS
    ns = seq // ts
    w_dn = DN_HEADS * DN_HD
    return pl.pallas_call(
        functools.partial(_dn_kernel, ts=ts),
        grid=(batch, ns),
        in_specs=[pl.BlockSpec((ts, 3 * w_dn), lambda b, s: (b * ns + s, COL_DN_QKV // (3 * w_dn))),
                  pl.BlockSpec((ts, w_dn), lambda b, s: (b * ns + s, COL_DN_Z // w_dn)),
                  pl.BlockSpec((ts, LANES), lambda b, s: (b * ns + s, COL_AB // LANES)),
                  pl.BlockSpec((DN_CONV, 3 * w_dn), lambda b, s: (0, 0)),
                  pl.BlockSpec((1, LANES), lambda b, s: (0, 0)),
                  pl.BlockSpec((1, LANES), lambda b, s: (0, 0)),
                  pl.BlockSpec((1, DN_HD), lambda b, s: (0, 0))],
        out_specs=pl.BlockSpec((ts, w_dn), lambda b, s: (b * ns + s, 0)),
        out_shape=jax.ShapeDtypeStruct((m, w_dn), BRANCH_DTYPE),
        scratch_shapes=[pltpu.VMEM((ts + 8, 3 * w_dn), F32),
                        pltpu.VMEM((ts, 3 * w_dn), F32),
                        pltpu.VMEM((DN_HEADS, DN_HD, DN_HD), F32)],
        compiler_params=_params("arbitrary", "arbitrary"),
        name="deltanet",
    )(proj, proj, proj, conv_w, a_log_pad, dt_pad, norm_g)


def _rope_table_kernel(pos_ref, inv_ref, cos_ref, sin_ref):
    ang = pos_ref[0] * inv_ref[...]
    cos_ref[...] = jnp.cos(ang)
    sin_ref[...] = jnp.sin(ang)


def _rope_table_call(pos_f, inv_lane):
    batch, seq, _ = pos_f.shape
    ts = RT_TS
    ns = seq // ts
    out = jax.ShapeDtypeStruct((batch * seq, LANES), F32)
    return pl.pallas_call(
        _rope_table_kernel,
        grid=(batch, ns),
        in_specs=[pl.BlockSpec((1, ts, 1), lambda b, s: (b, s, 0)),
                  pl.BlockSpec((1, LANES), lambda b, s: (0, 0))],
        out_specs=[pl.BlockSpec((ts, LANES), lambda b, s: (b * ns + s, 0))] * 2,
        out_shape=[out, out],
        compiler_params=_params("arbitrary", "arbitrary"),
        name="rope_table",
    )(pos_f, inv_lane)


def _ret_kernel(q_ref, k_ref, v_ref, z_ref, cos_ref, sin_ref, g_ref, o_ref, state_ref, *, ts):
    c = RET_CHUNK
    w_qk = RET_HEADS * RET_QK_HD
    half = RET_QK_HD // 2

    @pl.when(pl.program_id(1) == 0)
    def _():
        state_ref[...] = jnp.zeros_like(state_ref)

    lane = lax.broadcasted_iota(jnp.int32, (1, w_qk), 1)
    first_half = (lane[:, :LANES] % RET_QK_HD) < half
    cos, sin = cos_ref[...], sin_ref[...]

    def rope(t):
        parts = []
        for l0 in range(0, w_qk, LANES):
            tl = t[:, l0:l0 + LANES]
            upper = pltpu.roll(tl, LANES - half, 1)
            lower = pltpu.roll(tl, half, 1)
            parts.append(tl * cos + jnp.where(first_half, -upper, lower) * sin)
        return jnp.concatenate(parts, axis=1)

    q = rope(q_ref[...].astype(F32)) * (RET_QK_HD ** -0.5)
    k = rope(k_ref[...].astype(F32))

    ri = lax.broadcasted_iota(jnp.int32, (c, c), 0)
    ci = lax.broadcasted_iota(jnp.int32, (c, c), 1)
    rel = (ri - ci).astype(F32)
    idx = lax.broadcasted_iota(jnp.int32, (c, 1), 0).astype(F32)

    heads = range(RET_HEADS)
    log_gammas = [math.log1p(-(2.0 ** (-5.0 - h))) for h in heads]
    intras = [jnp.where(rel >= 0.0, jnp.exp(jnp.maximum(rel, 0.0) * lg), 0.0) for lg in log_gammas]
    crosses = [jnp.exp((idx + 1.0) * lg) for lg in log_gammas]
    sdecs = [jnp.exp((c - 1.0 - idx) * lg) for lg in log_gammas]
    in_heads = [(lane >= h * RET_QK_HD) & (lane < (h + 1) * RET_QK_HD) for h in heads]
    states = [state_ref[h] for h in heads]
    for ic in range(ts // c):
        rows = slice(ic * c, (ic + 1) * c)
        kc = k[rows]
        kcb = kc.astype(BF16)
        qhs = [jnp.where(in_heads[h], q[rows], 0.0).astype(BF16) for h in heads]
        vhs = [v_ref[rows, h * RET_V_HD:(h + 1) * RET_V_HD] for h in heads]
        scores = [_dot_nt(qhs[h], kcb) * intras[h] for h in heads]
        o_cross = [_dot(qhs[h], states[h]) * crosses[h] for h in heads]
        kds = [(jnp.where(in_heads[h], kc, 0.0) * sdecs[h]).T for h in heads]
        states = [states[h] * math.exp(c * log_gammas[h]) + _dot(kds[h], vhs[h]) for h in heads]
        outs = [_dot(scores[h], vhs[h]) + o_cross[h] for h in heads]
        for h in heads:
            o = outs[h]
            mu = jnp.mean(o, axis=-1, keepdims=True)
            var = jnp.mean(jnp.square(o - mu), axis=-1, keepdims=True)
            y = (o - mu) * lax.rsqrt(var + EPS) * g_ref[:, h * RET_V_HD:(h + 1) * RET_V_HD]
            zh = z_ref[rows, h * RET_V_HD:(h + 1) * RET_V_HD].astype(F32)
            o_ref[rows, h * RET_V_HD:(h + 1) * RET_V_HD] = (y * _silu(zh)).astype(o_ref.dtype)
    for h in heads:
        state_ref[h] = states[h]


def _ret_call(proj, cos, sin, norm_g, batch, seq):
    m = proj.shape[0]
    ts = RT_TS
    ns = seq // ts
    w_qk = RET_HEADS * RET_QK_HD
    w_v = RET_HEADS * RET_V_HD
    return pl.pallas_call(
        functools.partial(_ret_kernel, ts=ts),
        grid=(batch, ns),
        in_specs=[pl.BlockSpec((ts, w_qk), lambda b, s: (b * ns + s, COL_RT_Q // w_qk)),
                  pl.BlockSpec((ts, w_qk), lambda b, s: (b * ns + s, COL_RT_K // w_qk)),
                  pl.BlockSpec((ts, w_v), lambda b, s: (b * ns + s, COL_RT_V // w_v)),
                  pl.BlockSpec((ts, w_v), lambda b, s: (b * ns + s, COL_RT_Z // w_v)),
                  pl.BlockSpec((ts, LANES), lambda b, s: (b * ns + s, 0)),
                  pl.BlockSpec((ts, LANES), lambda b, s: (b * ns + s, 0)),
                  pl.BlockSpec((1, w_v), lambda b, s: (0, 0))],
        out_specs=pl.BlockSpec((ts, w_v), lambda b, s: (b * ns + s, 0)),
        out_shape=jax.ShapeDtypeStruct((m, w_v), BRANCH_DTYPE),
        scratch_shapes=[pltpu.VMEM((RET_HEADS, w_qk, RET_V_HD), F32)],
        compiler_params=_params("arbitrary", "arbitrary"),
        name="retention",
    )(proj, proj, proj, proj, cos, sin, norm_g)


def _mem_kernel(q_ref, mem_ref, mg_ref, wkv_ref, o_ref, kv_ref):
    w_mem = MEM_HEADS * MEM_HD

    @pl.when(pl.program_id(1) == 0)
    def _():
        mem_n = _rms(mem_ref[0], mg_ref[...])
        kv_ref[...] = _dot(mem_n, wkv_ref[...])

    lane = lax.broadcasted_iota(jnp.int32, (1, w_mem), 1)
    q = q_ref[...].astype(F32) * (MEM_HD ** -0.5)
    km = kv_ref[:, 0:w_mem]
    vm = kv_ref[:, w_mem:2 * w_mem]
    out = jnp.zeros(q.shape, F32)
    for h in range(MEM_HEADS):
        in_head = (lane >= h * MEM_HD) & (lane < (h + 1) * MEM_HD)
        scores = _dot_nt(jnp.where(in_head, q, 0.0), km)
        e = jnp.exp(scores - jnp.max(scores, axis=-1, keepdims=True))
        p = e / jnp.sum(e, axis=-1, keepdims=True)
        out = jnp.where(in_head, _dot(p, vm), out)
    o_ref[...] = out.astype(o_ref.dtype)


def _mem_call(proj, mem, mem_g, w_kv, batch, seq):
    m = proj.shape[0]
    ts = MEM_TS
    ns = seq // ts
    w_mem = MEM_HEADS * MEM_HD
    mem_len, d = mem.shape[1], mem.shape[2]
    return pl.pallas_call(
        _mem_kernel,
        grid=(batch, ns),
        in_specs=[pl.BlockSpec((ts, w_mem), lambda b, s: (b * ns + s, COL_MEM_Q // w_mem)),
                  pl.BlockSpec((1, mem_len, d), lambda b, s: (b, 0, 0)),
                  pl.BlockSpec((1, d), lambda b, s: (0, 0)),
                  pl.BlockSpec((d, 2 * w_mem), lambda b, s: (0, 0))],
        out_specs=pl.BlockSpec((ts, w_mem), lambda b, s: (b * ns + s, 0)),
        out_shape=jax.ShapeDtypeStruct((m, w_mem), BRANCH_DTYPE),
        scratch_shapes=[pltpu.VMEM((mem_len, 2 * w_mem), F32)],
        compiler_params=_params("arbitrary", "arbitrary"),
        name="mem_attn",
    )(proj, mem, mem_g, w_kv)


def _merge_kernel(x_ref, osb_ref, odn_ref, ort_ref, omem_ref, g_ref, wg_ref, bg_ref,
                  wsb_ref, wdn_ref, wrt_ref, wmem_ref, wout_ref, fg_ref, o_ref, *, final):
    d = x_ref.shape[1]
    x = x_ref[...]
    h = _rms(x, g_ref[...]).astype(BF16)
    merged = None
    branches = ((osb_ref, wsb_ref), (odn_ref, wdn_ref), (ort_ref, wrt_ref), (omem_ref, wmem_ref))
    for br, (ob_ref, wb_ref) in enumerate(branches):
        logits = jnp.dot(h, wg_ref[:, br * d:(br + 1) * d], preferred_element_type=F32)
        gate = jax.nn.sigmoid(logits + bg_ref[:, br * d:(br + 1) * d])
        term = gate * _dot(ob_ref[...], wb_ref[...])
        merged = term if merged is None else merged + term
    y = x + _dot(merged, wout_ref[...])
    if final:
        y = _rms(y, fg_ref[...])
    o_ref[...] = y


def _merge_call(xf, o_sb, o_dn, o_rt, o_mem, g, w_g, b_g, w_sb, w_dn, w_rt, w_mem, w_out, fg, final):
    m, d = xf.shape
    tm = MERGE_TM

    def rows(width):
        return pl.BlockSpec((tm, width), lambda i: (i, 0))

    def whole(a):
        return pl.BlockSpec(a.shape, lambda i: (0, 0), pipeline_mode=pl.Buffered(1))

    args = (xf, o_sb, o_dn, o_rt, o_mem, g, w_g, b_g, w_sb, w_dn, w_rt, w_mem, w_out, fg)
    in_specs = [rows(a.shape[1]) for a in args[:5]] + [whole(a) for a in args[5:]]
    return pl.pallas_call(
        functools.partial(_merge_kernel, final=final),
        grid=(m // tm,),
        in_specs=in_specs,
        out_specs=rows(d),
        out_shape=jax.ShapeDtypeStruct((m, d), F32),
        compiler_params=_params("arbitrary"),
        name="merge",
    )(*args)


def _arrange_w_in(w):
    sb = SB_HEADS * SB_HD
    dn = DN_HEADS * DN_HD
    o = 0
    sb_qkvz = w[:, o:o + 4 * sb]; o += 4 * sb
    dn_qkv = w[:, o:o + 3 * dn]; o += 3 * dn
    dn_z = w[:, o:o + dn]; o += dn
    ab = w[:, o:o + 2 * DN_HEADS]; o += 2 * DN_HEADS
    rt_q = w[:, o:o + 256]; o += 256
    rt_k = w[:, o:o + 256]; o += 256
    rt_v = w[:, o:o + 512]; o += 512
    rt_z = w[:, o:o + 512]; o += 512
    mem_q = w[:, o:o + 256]; o += 256
    gates = w[:, o:]
    ab_pad = jnp.pad(ab, ((0, 0), (0, LANES - 2 * DN_HEADS)))
    tail = jnp.zeros((w.shape[0], PROJ_COLS - COL_AB - LANES), w.dtype)
    w_a = jnp.concatenate([dn_qkv, dn_z, sb_qkvz, rt_v, rt_z, rt_q, rt_k, mem_q, ab_pad, tail], axis=1)
    return w_a.astype(BF16), gates.astype(BF16)


def kernel(x, mem, positions, norm_g, mem_norm_g, w_in, b_gate, dn_conv_w, dn_a_log, dn_dt_bias,
           dn_norm_g, ret_norm_g, w_mem_kv, w_br_sb, w_br_dn, w_br_ret, w_br_mem, w_out, final_norm_g):
    batch, seq, d = x.shape
    depth = w_in.shape[0]
    xf = x.reshape(batch * seq, d)
    pos_f = positions.astype(F32)[..., None]
    half = RET_QK_HD // 2
    inv = ROPE_BASE ** (-jnp.arange(half, dtype=F32) / half)
    inv_lane = jnp.tile(inv, LANES // half)[None, :]
    rope_cos, rope_sin = _rope_table_call(pos_f, inv_lane)
    lane_pad = LANES - DN_HEADS
    for l in range(depth):
        w_a, w_g = _arrange_w_in(w_in[l])
        proj = _proj_call(xf, norm_g[l][None, :], w_a)
        o_sb = _sb_call(proj, batch, seq)
        o_dn = _dn_call(proj, dn_conv_w[l],
                        jnp.pad(dn_a_log[l], (0, lane_pad))[None, :],
                        jnp.pad(dn_dt_bias[l], (0, lane_pad))[None, :],
                        dn_norm_g[l][None, :], batch, seq)
        o_rt = _ret_call(proj, rope_cos, rope_sin, ret_norm_g[l][None, :], batch, seq)
        o_mem = _mem_call(proj, mem, mem_norm_g[l][None, :], w_mem_kv[l].astype(BF16), batch, seq)
        xf = _merge_call(xf, o_sb, o_dn, o_rt, o_mem, norm_g[l][None, :], w_g, b_gate[l][None, :],
                         w_br_sb[l].astype(BF16), w_br_dn[l].astype(BF16), w_br_ret[l].astype(BF16),
                         w_br_mem[l].astype(BF16), w_out[l].astype(BF16), final_norm_g[None, :],
                         final=(l == depth - 1))
    return xf.reshape(batch, seq, d)
```

```python
import functools
import math

import jax
import jax.numpy as jnp
from jax import lax
from jax.experimental import pallas as pl
from jax.experimental.pallas import tpu as pltpu

F32 = jnp.float32
BF16 = jnp.bfloat16
EPS = 1e-6

D_MODEL = 1024
SB_HEADS, SB_HD = 8, 64
DN_HEADS, DN_HD, DN_CONV, DN_CHUNK = 4, 128, 4, 128
RET_HEADS, RET_QK_HD, RET_V_HD, RET_CHUNK = 4, 64, 128, 128
MEM_HEADS, MEM_HD = 4, 64
N_BRANCH = 4
ROPE_BASE = 10000.0
LOG2E = 1.4426950408889634

LANES = 128
VMEM_LIMIT = 56 << 20

COL_DN_QKV = 0
COL_DN_Z = 1536
COL_SB_Q = 2048
COL_SB_K = 2560
COL_SB_V = 3072
COL_SB_Z = 3584
COL_RT_V = 4096
COL_RT_Z = 4608
COL_RT_Q = 5120
COL_RT_K = 5376
COL_MEM_Q = 5632
COL_AB = 5888
PROJ_COLS = 6144

PROJ_TM, PROJ_TN = 1024, 512
SB_TK = 256
DN_TS = 512
RT_TS = 512
MEM_TS = 512
MERGE_TM = 512

PROJ_DTYPE = BF16
BRANCH_DTYPE = BF16


def _params(*sem):
    return pltpu.CompilerParams(dimension_semantics=sem, vmem_limit_bytes=VMEM_LIMIT)


def _rms(x, g):
    return x * lax.rsqrt(jnp.mean(x * x, axis=-1, keepdims=True) + EPS) * g


def _silu(x):
    half = 0.5 * x
    return half + half * jnp.tanh(half)


def _dot(a, b):
    return jnp.dot(a.astype(BF16), b.astype(BF16), preferred_element_type=F32)


def _dot_nt(a, b):
    return lax.dot_general(a.astype(BF16), b.astype(BF16), (((1,), (1,)), ((), ())),
                           preferred_element_type=F32)


def _proj_kernel(x_ref, g_ref, w_ref, o_ref):
    h = _rms(x_ref[...], g_ref[...]).astype(BF16)
    n = w_ref.shape[1]
    for c0 in range(0, n, PROJ_TN):
        o_ref[:, c0:c0 + PROJ_TN] = jnp.dot(h, w_ref[:, c0:c0 + PROJ_TN],
                                            preferred_element_type=F32).astype(o_ref.dtype)


def _proj_call(xf, g, w_a):
    m, d = xf.shape
    n = w_a.shape[1]
    return pl.pallas_call(
        _proj_kernel,
        grid=(m // PROJ_TM,),
        in_specs=[pl.BlockSpec((PROJ_TM, d), lambda i: (i, 0)),
                  pl.BlockSpec((1, d), lambda i: (0, 0), pipeline_mode=pl.Buffered(1)),
                  pl.BlockSpec((d, n), lambda i: (0, 0), pipeline_mode=pl.Buffered(1))],
        out_specs=pl.BlockSpec((PROJ_TM, n), lambda i: (i, 0)),
        out_shape=jax.ShapeDtypeStruct((m, n), PROJ_DTYPE),
        compiler_params=_params("arbitrary"),
        name="proj",
    )(xf, g, w_a)


def _sb_kernel(q_ref, k_ref, v_ref, z_ref, o_ref, *, tk):
    seq = q_ref.shape[0]
    nb = seq // tk
    lane = lax.broadcasted_iota(jnp.int32, (1, LANES), 1)
    row = lax.broadcasted_iota(jnp.int32, (tk, tk), 0)
    col = lax.broadcasted_iota(jnp.int32, (tk, tk), 1)
    tri = (row >= col).astype(BF16)
    diag_mask = col < row
    head_lanes = [(lane >= hh * SB_HD) & (lane < (hh + 1) * SB_HD) for hh in range(2)]

    def blk(r):
        return slice(r * tk, (r + 1) * tk)

    qms = []
    for r in range(nb):
        q = q_ref[blk(r), :].astype(F32) * (SB_HD ** -0.5 * LOG2E)
        qms.append([jnp.where(head_lanes[hh], q, 0.0).astype(BF16) for hh in range(2)])
    zero_c = jnp.zeros((tk, 1), F32)
    carry = {(r, hh): zero_c for r in range(nb) for hh in range(2)}
    acc = [jnp.zeros((tk, LANES), F32) for _ in range(nb)]

    def chains(c):
        return [(r, hh) for r in range(c, nb) for hh in range(2)]

    def scores(c):
        kb = k_ref[blk(c), :].astype(BF16)
        z2s = {ch: _dot_nt(qms[ch[0]][ch[1]], kb) for ch in chains(c)}
        sp2s = {}
        for ch, z2 in z2s.items():
            sp2 = jnp.maximum(z2, 0.0) + jnp.log2(1.0 + jnp.exp2(-jnp.abs(z2)))
            sp2s[ch] = (sp2 if ch[0] != c else jnp.where(diag_mask, sp2, 0.0)).astype(BF16)
        return z2s, sp2s

    def weights(c, z2s, sp2s):
        incls = {ch: jnp.dot(sp2s[ch], tri, preferred_element_type=F32) for ch in chains(c)}
        ws = {}
        for ch in chains(c):
            w = jnp.exp2(z2s[ch] - incls[ch] - carry[ch])
            ws[ch] = (w if ch[0] != c else jnp.where(diag_mask, w, 0.0)).astype(BF16)
            carry[ch] = carry[ch] + incls[ch][:, 0:1]
        return ws

    def accumulate(c, ws):
        vb = v_ref[blk(c), :]
        vh = [jnp.where(head_lanes[hh], vb, jnp.zeros_like(vb)).astype(BF16) for hh in range(2)]
        for r in range(c, nb):
            acc[r] = (acc[r] + jnp.dot(ws[(r, 0)], vh[0], preferred_element_type=F32)
                      + jnp.dot(ws[(r, 1)], vh[1], preferred_element_type=F32))

    pending = scores(nb - 1)
    for c in range(nb - 1, -1, -1):
        ahead = scores(c - 1) if c > 0 else None
        accumulate(c, weights(c, *pending))
        pending = ahead
    for r in range(nb):
        o_ref[blk(r), :] = (acc[r] * _silu(z_ref[blk(r), :].astype(F32))).astype(o_ref.dtype)


def _sb_call(proj, batch, seq):
    m = proj.shape[0]
    pairs = SB_HEADS * SB_HD // LANES
    cq, ck, cv, cz = (c // LANES for c in (COL_SB_Q, COL_SB_K, COL_SB_V, COL_SB_Z))
    return pl.pallas_call(
        functools.partial(_sb_kernel, tk=SB_TK),
        grid=(batch, pairs),
        in_specs=[pl.BlockSpec((seq, LANES), lambda b, p: (b, cq + p)),
                  pl.BlockSpec((seq, LANES), lambda b, p: (b, ck + p)),
                  pl.BlockSpec((seq, LANES), lambda b, p: (b, cv + p)),
                  pl.BlockSpec((seq, LANES), lambda b, p: (b, cz + p))],
        out_specs=pl.BlockSpec((seq, LANES), lambda b, p: (b, p)),
        out_shape=jax.ShapeDtypeStruct((m, SB_HEADS * SB_HD), BRANCH_DTYPE),
        compiler_params=_params("arbitrary", "arbitrary"),
        name="sb_attn",
    )(proj, proj, proj, proj)


def _level_masks(c):
    ri = lax.broadcasted_iota(jnp.int32, (c, c), 0)
    ci = lax.broadcasted_iota(jnp.int32, (c, c), 1)
    masks = [((ri == ci + 1) & ((ri & 1) == 1)).astype(F32)]
    b = 2
    while b < c:
        shift = int(math.log2(2 * b))
        masks.append((((ri >> shift) == (ci >> shift)) & ((ri & b) != 0) & ((ci & b) == 0)).astype(F32))
        b *= 2
    return masks


def _unit_lower_inverses(lows, masks, eye):
    invs = [eye - low * masks[0] for low in lows]
    for m in masks[1:]:
        inner = [_dot(low * m, inv) for low, inv in zip(lows, invs)]
        invs = [inv - _dot(inv, x) for inv, x in zip(invs, inner)]
    return invs


def _dn_kernel(qkv_ref, z_ref, ab_ref, cw_ref, alog_ref, dt_ref, ng_ref, o_ref,
               ext_ref, xc_ref, state_ref, *, ts):
    c = DN_CHUNK
    w_dn = DN_HEADS * DN_HD
    pad = 8

    @pl.when(pl.program_id(1) == 0)
    def _():
        ext_ref[0:pad, :] = jnp.zeros((pad, 3 * w_dn), F32)
        state_ref[...] = jnp.zeros_like(state_ref)

    @pl.when(pl.program_id(1) > 0)
    def _():
        ext_ref[0:pad, :] = ext_ref[ts:ts + pad, :]

    ext_ref[pad:pad + ts, :] = qkv_ref[...].astype(F32)

    for col0 in range(0, 3 * w_dn, LANES):
        acc = None
        for j in range(DN_CONV):
            lo = pad - (DN_CONV - 1) + j
            term = ext_ref[lo:lo + ts, col0:col0 + LANES] * cw_ref[j:j + 1, col0:col0 + LANES]
            acc = term if acc is None else acc + term
        xc_ref[:, col0:col0 + LANES] = _silu(acc)

    ri = lax.broadcasted_iota(jnp.int32, (c, c), 0)
    ci = lax.broadcasted_iota(jnp.int32, (c, c), 1)
    tril = ri >= ci
    strict = ri > ci
    tril_b = tril.astype(BF16)

    def l2n(t):
        return t * lax.rsqrt(jnp.sum(t * t, axis=-1, keepdims=True) + EPS)

    masks = _level_masks(c)
    eye = (ri == ci).astype(F32)
    heads = range(DN_HEADS)

    def local_part(ic):
        rows = slice(ic * c, (ic + 1) * c)
        ab = ab_ref[rows, :].astype(F32)
        pre = ab + dt_ref[...]
        sp = jnp.maximum(pre, 0.0) + jnp.log(1.0 + jnp.exp(-jnp.abs(pre)))
        g_all = -jnp.exp(alog_ref[...]) * sp
        beta_all = jax.nn.sigmoid(ab)
        g_hi = g_all.astype(BF16)
        g_lo = (g_all - g_hi.astype(F32)).astype(BF16)
        gc_all = (jnp.dot(tril_b, g_hi, preferred_element_type=F32)
                  + jnp.dot(tril_b, g_lo, preferred_element_type=F32))
        gc_t = gc_all.T
        qs = [l2n(xc_ref[rows, h * DN_HD:(h + 1) * DN_HD]) * (DN_HD ** -0.5) for h in heads]
        ks = [l2n(xc_ref[rows, w_dn + h * DN_HD:w_dn + (h + 1) * DN_HD]) for h in heads]
        vs = [xc_ref[rows, 2 * w_dn + h * DN_HD:2 * w_dn + (h + 1) * DN_HD] for h in heads]
        gcols = [gc_all[:, h:h + 1] for h in heads]
        betas = [beta_all[:, DN_HEADS + h:DN_HEADS + h + 1] for h in heads]
        decays = [jnp.where(tril, jnp.exp(jnp.minimum(gcols[h] - gc_t[h:h + 1, :], 0.0)), 0.0) for h in heads]
        k_betas = [ks[h] * betas[h] for h in heads]
        kqs = [_dot_nt(jnp.concatenate([k_betas[h], qs[h]], axis=0), ks[h]) for h in heads]
        lows = [jnp.where(strict, kqs[h][:c] * decays[h], 0.0) for h in heads]
        a_intras = [jnp.where(tril, kqs[h][c:] * decays[h], 0.0) for h in heads]
        egcs = [jnp.exp(gcols[h]) for h in heads]
        rhs = [jnp.concatenate([vs[h] * betas[h], k_betas[h] * egcs[h]], axis=1) for h in heads]
        q_egcs = [qs[h] * egcs[h] for h in heads]
        g_lasts = [gc_all[c - 1:c, h:h + 1] for h in heads]
        k_decs = [ks[h] * jnp.exp(g_lasts[h] - gcols[h]) for h in heads]
        e_lasts = [jnp.exp(g_lasts[h]) for h in heads]
        return lows, rhs, q_egcs, a_intras, k_decs, e_lasts

    n_chunks = ts // c
    parts = [local_part(ic) for ic in range(n_chunks)]
    t_invs = _unit_lower_inverses([low for p in parts for low in p[0]], masks, eye)
    uws = [_dot(t, r) for t, r in zip(t_invs, [r for p in parts for r in p[1]])]

    states = [state_ref[h] for h in heads]
    for ic in range(n_chunks):
        rows = slice(ic * c, (ic + 1) * c)
        _, _, q_egcs, a_intras, k_decs, e_lasts = parts[ic]
        uw = uws[ic * DN_HEADS:(ic + 1) * DN_HEADS]
        wss = [_dot(jnp.concatenate([uw[h][:, DN_HD:], q_egcs[h]], axis=0), states[h])
               for h in heads]
        v_news = [uw[h][:, :DN_HD] - wss[h][:c] for h in heads]
        states = [states[h] * e_lasts[h] + lax.dot_general(
            k_decs[h].astype(BF16), v_news[h].astype(BF16), (((0,), (0,)), ((), ())),
            preferred_element_type=F32) for h in heads]
        outs = [wss[h][c:] + _dot(a_intras[h], v_news[h]) for h in heads]
        for h in heads:
            cols = slice(h * DN_HD, (h + 1) * DN_HD)
            o_ref[rows, cols] = (_rms(outs[h], ng_ref[...])
                                 * _silu(z_ref[rows, cols].astype(F32))).astype(o_ref.dtype)
    for h in heads:
        state_ref[h] = states[h]


def _dn_call(proj, conv_w, a_log_pad, dt_pad, norm_g, batch, seq):
    m = proj.shape[0]
    ts = DN_TS
    ns = seq // ts
    w_dn = DN_HEADS * DN_HD
    return pl.pallas_call(
        functools.partial(_dn_kernel, ts=ts),
        grid=(batch, ns),
        in_specs=[pl.BlockSpec((ts, 3 * w_dn), lambda b, s: (b * ns + s, COL_DN_QKV // (3 * w_dn))),
                  pl.BlockSpec((ts, w_dn), lambda b, s: (b * ns + s, COL_DN_Z // w_dn)),
                  pl.BlockSpec((ts, LANES), lambda b, s: (b * ns + s, COL_AB // LANES)),
                  pl.BlockSpec((DN_CONV, 3 * w_dn), lambda b, s: (0, 0)),
                  pl.BlockSpec((1, LANES), lambda b, s: (0, 0)),
                  pl.BlockSpec((1, LANES), lambda b, s: (0, 0)),
                  pl.BlockSpec((1, DN_HD), lambda b, s: (0, 0))],
        out_specs=pl.BlockSpec((ts, w_dn), lambda b, s: (b * ns + s, 0)),
        out_shape=jax.ShapeDtypeStruct((m, w_dn), BRANCH_DTYPE),
        scratch_shapes=[pltpu.VMEM((ts + 8, 3 * w_dn), F32),
                        pltpu.VMEM((ts, 3 * w_dn), F32),
                        pltpu.VMEM((DN_HEADS, DN_HD, DN_HD), F32)],
        compiler_params=_params("arbitrary", "arbitrary"),
        name="deltanet",
    )(proj, proj, proj, conv_w, a_log_pad, dt_pad, norm_g)


def _rope_table_kernel(pos_ref, inv_ref, cos_ref, sin_ref):
    ang = pos_ref[0] * inv_ref[...]
    cos_ref[...] = jnp.cos(ang)
    sin_ref[...] = jnp.sin(ang)


def _rope_table_call(pos_f, inv_lane):
    batch, seq, _ = pos_f.shape
    ts = RT_TS
    ns = seq // ts
    out = jax.ShapeDtypeStruct((batch * seq, LANES), F32)
    return pl.pallas_call(
        _rope_table_kernel,
        grid=(batch, ns),
        in_specs=[pl.BlockSpec((1, ts, 1), lambda b, s: (b, s, 0)),
                  pl.BlockSpec((1, LANES), lambda b, s: (0, 0))],
        out_specs=[pl.BlockSpec((ts, LANES), lambda b, s: (b * ns + s, 0))] * 2,
        out_shape=[out, out],
        compiler_params=_params("arbitrary", "arbitrary"),
        name="rope_table",
    )(pos_f, inv_lane)


def _ret_kernel(q_ref, k_ref, v_ref, z_ref, cos_ref, sin_ref, g_ref, o_ref, state_ref, *, ts):
    c = RET_CHUNK
    w_qk = RET_HEADS * RET_QK_HD
    half = RET_QK_HD // 2

    @pl.when(pl.program_id(1) == 0)
    def _():
        state_ref[...] = jnp.zeros_like(state_ref)

    lane = lax.broadcasted_iota(jnp.int32, (1, w_qk), 1)
    first_half = (lane[:, :LANES] % RET_QK_HD) < half
    cos, sin = cos_ref[...], sin_ref[...]

    def rope(t):
        parts = []
        for l0 in range(0, w_qk, LANES):
            tl = t[:, l0:l0 + LANES]
            upper = pltpu.roll(tl, LANES - half, 1)
            lower = pltpu.roll(tl, half, 1)
            parts.append(tl * cos + jnp.where(first_half, -upper, lower) * sin)
        return jnp.concatenate(parts, axis=1)

    q = rope(q_ref[...].astype(F32)) * (RET_QK_HD ** -0.5)
    k = rope(k_ref[...].astype(F32))

    ri = lax.broadcasted_iota(jnp.int32, (c, c), 0)
    ci = lax.broadcasted_iota(jnp.int32, (c, c), 1)
    rel = (ri - ci).astype(F32)
    idx = lax.broadcasted_iota(jnp.int32, (c, 1), 0).astype(F32)

    heads = range(RET_HEADS)
    log_gammas = [math.log1p(-(2.0 ** (-5.0 - h))) for h in heads]
    intras = [jnp.where(rel >= 0.0, jnp.exp(jnp.maximum(rel, 0.0) * lg), 0.0) for lg in log_gammas]
    crosses = [jnp.exp((idx + 1.0) * lg) for lg in log_gammas]
    sdecs = [jnp.exp((c - 1.0 - idx) * lg) for lg in log_gammas]
    in_heads = [(lane >= h * RET_QK_HD) & (lane < (h + 1) * RET_QK_HD) for h in heads]
    states = [state_ref[h] for h in heads]
    for ic in range(ts // c):
        rows = slice(ic * c, (ic + 1) * c)
        kc = k[rows]
        kcb = kc.astype(BF16)
        qhs = [jnp.where(in_heads[h], q[rows], 0.0).astype(BF16) for h in heads]
        vhs = [v_ref[rows, h * RET_V_HD:(h + 1) * RET_V_HD] for h in heads]
        scores = [_dot_nt(qhs[h], kcb) * intras[h] for h in heads]
        o_cross = [_dot(qhs[h], states[h]) * crosses[h] for h in heads]
        kds = [(jnp.where(in_heads[h], kc, 0.0) * sdecs[h]).T for h in heads]
        states = [states[h] * math.exp(c * log_gammas[h]) + _dot(kds[h], vhs[h]) for h in heads]
        outs = [_dot(scores[h], vhs[h]) + o_cross[h] for h in heads]
        for h in heads:
            o = outs[h]
            mu = jnp.mean(o, axis=-1, keepdims=True)
            var = jnp.mean(jnp.square(o - mu), axis=-1, keepdims=True)
            y = (o - mu) * lax.rsqrt(var + EPS) * g_ref[:, h * RET_V_HD:(h + 1) * RET_V_HD]
            zh = z_ref[rows, h * RET_V_HD:(h + 1) * RET_V_HD].astype(F32)
            o_ref[rows, h * RET_V_HD:(h + 1) * RET_V_HD] = (y * _silu(zh)).astype(o_ref.dtype)
    for h in heads:
        state_ref[h] = states[h]


def _ret_call(proj, cos, sin, norm_g, batch, seq):
    m = proj.shape[0]
    ts = RT_TS
    ns = seq // ts
    w_qk = RET_HEADS * RET_QK_HD
    w_v = RET_HEADS * RET_V_HD
    return pl.pallas_call(
        functools.partial(_ret_kernel, ts=ts),
        grid=(batch, ns),
        in_specs=[pl.BlockSpec((ts, w_qk), lambda b, s: (b * ns + s, COL_RT_Q // w_qk)),
                  pl.BlockSpec((ts, w_qk), lambda b, s: (b * ns + s, COL_RT_K // w_qk)),
                  pl.BlockSpec((ts, w_v), lambda b, s: (b * ns + s, COL_RT_V // w_v)),
                  pl.BlockSpec((ts, w_v), lambda b, s: (b * ns + s, COL_RT_Z // w_v)),
                  pl.BlockSpec((ts, LANES), lambda b, s: (b * ns + s, 0)),
                  pl.BlockSpec((ts, LANES), lambda b, s: (b * ns + s, 0)),
                  pl.BlockSpec((1, w_v), lambda b, s: (0, 0))],
        out_specs=pl.BlockSpec((ts, w_v), lambda b, s: (b * ns + s, 0)),
        out_shape=jax.ShapeDtypeStruct((m, w_v), BRANCH_DTYPE),
        scratch_shapes=[pltpu.VMEM((RET_HEADS, w_qk, RET_V_HD), F32)],
        compiler_params=_params("arbitrary", "arbitrary"),
        name="retention",
    )(proj, proj, proj, proj, cos, sin, norm_g)


def _mem_kernel(q_ref, mem_ref, mg_ref, wkv_ref, o_ref, kv_ref):
    w_mem = MEM_HEADS * MEM_HD

    @pl.when(pl.program_id(1) == 0)
    def _():
        mem_n = _rms(mem_ref[0], mg_ref[...])
        kv_ref[...] = _dot(mem_n, wkv_ref[...])

    lane = lax.broadcasted_iota(jnp.int32, (1, w_mem), 1)
    q = q_ref[...].astype(F32) * (MEM_HD ** -0.5)
    km = kv_ref[:, 0:w_mem]
    vm = kv_ref[:, w_mem:2 * w_mem]
    out = jnp.zeros(q.shape, F32)
    for h in range(MEM_HEADS):
        in_head = (lane >= h * MEM_HD) & (lane < (h + 1) * MEM_HD)
        scores = _dot_nt(jnp.where(in_head, q, 0.0), km)
        e = jnp.exp(scores - jnp.max(scores, axis=-1, keepdims=True))
        p = e / jnp.sum(e, axis=-1, keepdims=True)
        out = jnp.where(in_head, _dot(p, vm), out)
    o_ref[...] = out.astype(o_ref.dtype)


def _mem_call(proj, mem, mem_g, w_kv, batch, seq):
    m = proj.shape[0]
    ts = MEM_TS
    ns = seq // ts
    w_mem = MEM_HEADS * MEM_HD
    mem_len, d = mem.shape[1], mem.shape[2]
    return pl.pallas_call(
        _mem_kernel,
        grid=(batch, ns),
        in_specs=[pl.BlockSpec((ts, w_mem), lambda b, s: (b * ns + s, COL_MEM_Q // w_mem)),
                  pl.BlockSpec((1, mem_len, d), lambda b, s: (b, 0, 0)),
                  pl.BlockSpec((1, d), lambda b, s: (0, 0)),
                  pl.BlockSpec((d, 2 * w_mem), lambda b, s: (0, 0))],
        out_specs=pl.BlockSpec((ts, w_mem), lambda b, s: (b * ns + s, 0)),
        out_shape=jax.ShapeDtypeStruct((m, w_mem), BRANCH_DTYPE),
        scratch_shapes=[pltpu.VMEM((mem_len, 2 * w_mem), F32)],
        compiler_params=_params("arbitrary", "arbitrary"),
        name="mem_attn",
    )(proj, mem, mem_g, w_kv)


def _merge_kernel(x_ref, osb_ref, odn_ref, ort_ref, omem_ref, g_ref, wg_ref, bg_ref,
                  wsb_ref, wdn_ref, wrt_ref, wmem_ref, wout_ref, fg_ref, o_ref, *, final):
    d = x_ref.shape[1]
    x = x_ref[...]
    h = _rms(x, g_ref[...]).astype(BF16)
    merged = None
    branches = ((osb_ref, wsb_ref), (odn_ref, wdn_ref), (ort_ref, wrt_ref), (omem_ref, wmem_ref))
    for br, (ob_ref, wb_ref) in enumerate(branches):
        logits = jnp.dot(h, wg_ref[:, br * d:(br + 1) * d], preferred_element_type=F32)
        gate = jax.nn.sigmoid(logits + bg_ref[:, br * d:(br + 1) * d])
        term = gate * _dot(ob_ref[...], wb_ref[...])
        merged = term if merged is None else merged + term
    y = x + _dot(merged, wout_ref[...])
    if final:
        y = _rms(y, fg_ref[...])
    o_ref[...] = y


def _merge_call(xf, o_sb, o_dn, o_rt, o_mem, g, w_g, b_g, w_sb, w_dn, w_rt, w_mem, w_out, fg, final):
    m, d = xf.shape
    tm = MERGE_TM

    def rows(width):
        return pl.BlockSpec((tm, width), lambda i: (i, 0))

    def whole(a):
        return pl.BlockSpec(a.shape, lambda i: (0, 0), pipeline_mode=pl.Buffered(1))

    args = (xf, o_sb, o_dn, o_rt, o_mem, g, w_g, b_g, w_sb, w_dn, w_rt, w_mem, w_out, fg)
    in_specs = [rows(a.shape[1]) for a in args[:5]] + [whole(a) for a in args[5:]]
    return pl.pallas_call(
        functools.partial(_merge_kernel, final=final),
        grid=(m // tm,),
        in_specs=in_specs,
        out_specs=rows(d),
        out_shape=jax.ShapeDtypeStruct((m, d), F32),
        compiler_params=_params("arbitrary"),
        name="merge",
    )(*args)


def _arrange_w_in(w):
    sb = SB_HEADS * SB_HD
    dn = DN_HEADS * DN_HD
    widths = (4 * sb, 3 * dn, dn, 2 * DN_HEADS, 256, 256, 512, 512, 256)
    offs = [0]
    for wd in widths:
        offs.append(offs[-1] + wd)
    sb_qkvz, dn_qkv, dn_z, ab, rt_q, rt_k, rt_v, rt_z, mem_q = (
        w[..., offs[i]:offs[i + 1]].astype(BF16) for i in range(len(widths)))
    gates = w[..., offs[-1]:].astype(BF16)
    tail = jnp.zeros(w.shape[:-1] + (PROJ_COLS - COL_AB - 2 * DN_HEADS,), BF16)
    w_a = jnp.concatenate([dn_qkv, dn_z, sb_qkvz, rt_v, rt_z, rt_q, rt_k, mem_q, ab, tail], axis=-1)
    return w_a, gates


def kernel(x, mem, positions, norm_g, mem_norm_g, w_in, b_gate, dn_conv_w, dn_a_log, dn_dt_bias,
           dn_norm_g, ret_norm_g, w_mem_kv, w_br_sb, w_br_dn, w_br_ret, w_br_mem, w_out, final_norm_g):
    batch, seq, d = x.shape
    depth = w_in.shape[0]
    xf = x.reshape(batch * seq, d)
    pos_f = positions.astype(F32)[..., None]
    half = RET_QK_HD // 2
    inv = ROPE_BASE ** (-jnp.arange(half, dtype=F32) / half)
    inv_lane = jnp.tile(inv, LANES // half)[None, :]
    rope_cos, rope_sin = _rope_table_call(pos_f, inv_lane)
    lane_pad = ((0, 0), (0, LANES - DN_HEADS))
    a_log_pad = jnp.pad(dn_a_log, lane_pad)
    dt_pad = jnp.pad(dn_dt_bias, lane_pad)
    w_a, w_g = _arrange_w_in(w_in)
    w_kv, w_sb, w_dn, w_rt, w_mem, w_o = (t.astype(BF16) for t in (
        w_mem_kv, w_br_sb, w_br_dn, w_br_ret, w_br_mem, w_out))
    for l in range(depth):
        proj = _proj_call(xf, norm_g[l][None, :], w_a[l])
        o_sb = _sb_call(proj, batch, seq)
        o_dn = _dn_call(proj, dn_conv_w[l], a_log_pad[l][None, :], dt_pad[l][None, :],
                        dn_norm_g[l][None, :], batch, seq)
        o_rt = _ret_call(proj, rope_cos, rope_sin, ret_norm_g[l][None, :], batch, seq)
        o_mem = _mem_call(proj, mem, mem_norm_g[l][None, :], w_kv[l], batch, seq)
        xf = _merge_call(xf, o_sb, o_dn, o_rt, o_mem, norm_g[l][None, :], w_g[l], b_gate[l][None, :],
                         w_sb[l], w_dn[l], w_rt[l], w_mem[l], w_o[l], final_norm_g[None, :],
                         final=(l == depth - 1))
    return xf.reshape(batch, seq, d)
```

```python
import functools
import math

import jax
import jax.numpy as jnp
from jax import lax
from jax.experimental import pallas as pl
from jax.experimental.pallas import tpu as pltpu

F32 = jnp.float32
BF16 = jnp.bfloat16
EPS = 1e-6

D_MODEL = 1024
SB_HEADS, SB_HD = 8, 64
DN_HEADS, DN_HD, DN_CONV, DN_CHUNK = 4, 128, 4, 128
RET_HEADS, RET_QK_HD, RET_V_HD, RET_CHUNK = 4, 64, 128, 128
MEM_HEADS, MEM_HD = 4, 64
N_BRANCH = 4
ROPE_BASE = 10000.0
LOG2E = 1.4426950408889634

LANES = 128
VMEM_LIMIT = 56 << 20

COL_DN_QKV = 0
COL_DN_Z = 1536
COL_SB_Q = 2048
COL_SB_K = 2560
COL_SB_V = 3072
COL_SB_Z = 3584
COL_RT_V = 4096
COL_RT_Z = 4608
COL_RT_Q = 5120
COL_RT_K = 5376
COL_MEM_Q = 5632
COL_AB = 5888
PROJ_COLS = 6144

PROJ_TM, PROJ_TN = 1024, 512
SB_TK = 256
DN_TS = 512
RT_TS = 512
MEM_TS = 512
MERGE_TM = 512

PROJ_DTYPE = BF16
BRANCH_DTYPE = BF16


def _params(*sem):
    return pltpu.CompilerParams(dimension_semantics=sem, vmem_limit_bytes=VMEM_LIMIT)


def _rms(x, g):
    return x * lax.rsqrt(jnp.mean(x * x, axis=-1, keepdims=True) + EPS) * g


def _silu(x):
    half = 0.5 * x
    return half + half * jnp.tanh(half)


def _dot(a, b):
    return jnp.dot(a.astype(BF16), b.astype(BF16), preferred_element_type=F32)


def _dot_nt(a, b):
    return lax.dot_general(a.astype(BF16), b.astype(BF16), (((1,), (1,)), ((), ())),
                           preferred_element_type=F32)


def _proj_kernel(x_ref, g_ref, w_ref, o_ref):
    h = _rms(x_ref[...], g_ref[...]).astype(BF16)
    n = w_ref.shape[1]
    for c0 in range(0, n, PROJ_TN):
        o_ref[:, c0:c0 + PROJ_TN] = jnp.dot(h, w_ref[:, c0:c0 + PROJ_TN],
                                            preferred_element_type=F32).astype(o_ref.dtype)


def _proj_call(xf, g, w_a, layer):
    m, d = xf.shape
    n = w_a.shape[2]
    return pl.pallas_call(
        _proj_kernel,
        grid=(m // PROJ_TM,),
        in_specs=[pl.BlockSpec((PROJ_TM, d), lambda i: (i, 0)),
                  pl.BlockSpec((None, 1, d), lambda i: (layer, 0, 0), pipeline_mode=pl.Buffered(1)),
                  pl.BlockSpec((None, d, n), lambda i: (layer, 0, 0), pipeline_mode=pl.Buffered(1))],
        out_specs=pl.BlockSpec((PROJ_TM, n), lambda i: (i, 0)),
        out_shape=jax.ShapeDtypeStruct((m, n), PROJ_DTYPE),
        compiler_params=_params("arbitrary"),
        name="proj",
    )(xf, g, w_a)


def _sb_kernel(q_ref, k_ref, v_ref, z_ref, o_ref, *, tk):
    seq = q_ref.shape[0]
    nb = seq // tk
    lane = lax.broadcasted_iota(jnp.int32, (1, LANES), 1)
    row = lax.broadcasted_iota(jnp.int32, (tk, tk), 0)
    col = lax.broadcasted_iota(jnp.int32, (tk, tk), 1)
    tri = (row >= col).astype(BF16)
    diag_mask = col < row
    head_lanes = [(lane >= hh * SB_HD) & (lane < (hh + 1) * SB_HD) for hh in range(2)]

    def blk(r):
        return slice(r * tk, (r + 1) * tk)

    qms = []
    for r in range(nb):
        q = q_ref[blk(r), :].astype(F32) * (SB_HD ** -0.5 * LOG2E)
        qms.append([jnp.where(head_lanes[hh], q, 0.0).astype(BF16) for hh in range(2)])
    zero_c = jnp.zeros((tk, 1), F32)
    carry = {(r, hh): zero_c for r in range(nb) for hh in range(2)}
    acc = [jnp.zeros((tk, LANES), F32) for _ in range(nb)]

    def chains(c):
        return [(r, hh) for r in range(c, nb) for hh in range(2)]

    def scores(c):
        kb = k_ref[blk(c), :].astype(BF16)
        z2s = {ch: _dot_nt(qms[ch[0]][ch[1]], kb) for ch in chains(c)}
        sp2s = {}
        for ch, z2 in z2s.items():
            sp2 = jnp.maximum(z2, 0.0) + jnp.log2(1.0 + jnp.exp2(-jnp.abs(z2)))
            sp2s[ch] = (sp2 if ch[0] != c else jnp.where(diag_mask, sp2, 0.0)).astype(BF16)
        return z2s, sp2s

    def weights(c, z2s, sp2s):
        incls = {ch: jnp.dot(sp2s[ch], tri, preferred_element_type=F32) for ch in chains(c)}
        ws = {}
        for ch in chains(c):
            w = jnp.exp2(z2s[ch] - incls[ch] - carry[ch])
            ws[ch] = (w if ch[0] != c else jnp.where(diag_mask, w, 0.0)).astype(BF16)
            carry[ch] = carry[ch] + incls[ch][:, 0:1]
        return ws

    def accumulate(c, ws):
        vb = v_ref[blk(c), :]
        vh = [jnp.where(head_lanes[hh], vb, jnp.zeros_like(vb)).astype(BF16) for hh in range(2)]
        for r in range(c, nb):
            acc[r] = (acc[r] + jnp.dot(ws[(r, 0)], vh[0], preferred_element_type=F32)
                      + jnp.dot(ws[(r, 1)], vh[1], preferred_element_type=F32))

    pending = scores(nb - 1)
    for c in range(nb - 1, -1, -1):
        ahead = scores(c - 1) if c > 0 else None
        accumulate(c, weights(c, *pending))
        pending = ahead
    for r in range(nb):
        o_ref[blk(r), :] = (acc[r] * _silu(z_ref[blk(r), :].astype(F32))).astype(o_ref.dtype)


def _sb_call(proj, batch, seq):
    m = proj.shape[0]
    pairs = SB_HEADS * SB_HD // LANES
    cq, ck, cv, cz = (c // LANES for c in (COL_SB_Q, COL_SB_K, COL_SB_V, COL_SB_Z))
    return pl.pallas_call(
        functools.partial(_sb_kernel, tk=SB_TK),
        grid=(batch, pairs),
        in_specs=[pl.BlockSpec((seq, LANES), lambda b, p: (b, cq + p)),
                  pl.BlockSpec((seq, LANES), lambda b, p: (b, ck + p)),
                  pl.BlockSpec((seq, LANES), lambda b, p: (b, cv + p)),
                  pl.BlockSpec((seq, LANES), lambda b, p: (b, cz + p))],
        out_specs=pl.BlockSpec((seq, LANES), lambda b, p: (b, p)),
        out_shape=jax.ShapeDtypeStruct((m, SB_HEADS * SB_HD), BRANCH_DTYPE),
        compiler_params=_params("arbitrary", "arbitrary"),
        name="sb_attn",
    )(proj, proj, proj, proj)


def _level_masks(c):
    ri = lax.broadcasted_iota(jnp.int32, (c, c), 0)
    ci = lax.broadcasted_iota(jnp.int32, (c, c), 1)
    masks = [((ri == ci + 1) & ((ri & 1) == 1)).astype(F32)]
    b = 2
    while b < c:
        shift = int(math.log2(2 * b))
        masks.append((((ri >> shift) == (ci >> shift)) & ((ri & b) != 0) & ((ci & b) == 0)).astype(F32))
        b *= 2
    return masks


def _unit_lower_inverses(lows, masks, eye):
    invs = [eye - low * masks[0] for low in lows]
    for m in masks[1:]:
        inner = [_dot(low * m, inv) for low, inv in zip(lows, invs)]
        invs = [inv - _dot(inv, x) for inv, x in zip(invs, inner)]
    return invs


def _dn_kernel(qkv_ref, z_ref, ab_ref, cw_ref, alog_ref, dt_ref, ng_ref, o_ref,
               ext_ref, xc_ref, state_ref, *, ts):
    c = DN_CHUNK
    w_dn = DN_HEADS * DN_HD
    pad = 8

    @pl.when(pl.program_id(1) == 0)
    def _():
        ext_ref[0:pad, :] = jnp.zeros((pad, 3 * w_dn), F32)
        state_ref[...] = jnp.zeros_like(state_ref)

    @pl.when(pl.program_id(1) > 0)
    def _():
        ext_ref[0:pad, :] = ext_ref[ts:ts + pad, :]

    ext_ref[pad:pad + ts, :] = qkv_ref[...].astype(F32)

    for col0 in range(0, 3 * w_dn, LANES):
        acc = None
        for j in range(DN_CONV):
            lo = pad - (DN_CONV - 1) + j
            term = ext_ref[lo:lo + ts, col0:col0 + LANES] * cw_ref[j:j + 1, col0:col0 + LANES]
            acc = term if acc is None else acc + term
        xc_ref[:, col0:col0 + LANES] = _silu(acc)

    ri = lax.broadcasted_iota(jnp.int32, (c, c), 0)
    ci = lax.broadcasted_iota(jnp.int32, (c, c), 1)
    tril = ri >= ci
    strict = ri > ci
    tril_b = tril.astype(BF16)

    def l2n(t):
        return t * lax.rsqrt(jnp.sum(t * t, axis=-1, keepdims=True) + EPS)

    masks = _level_masks(c)
    eye = (ri == ci).astype(F32)
    heads = range(DN_HEADS)

    def local_part(ic):
        rows = slice(ic * c, (ic + 1) * c)
        ab = ab_ref[rows, :].astype(F32)
        pre = ab + dt_ref[...]
        sp = jnp.maximum(pre, 0.0) + jnp.log(1.0 + jnp.exp(-jnp.abs(pre)))
        g_all = -jnp.exp(alog_ref[...]) * sp
        beta_all = jax.nn.sigmoid(ab)
        g_hi = g_all.astype(BF16)
        g_lo = (g_all - g_hi.astype(F32)).astype(BF16)
        gc_all = (jnp.dot(tril_b, g_hi, preferred_element_type=F32)
                  + jnp.dot(tril_b, g_lo, preferred_element_type=F32))
        gc_t = gc_all.T
        qs = [l2n(xc_ref[rows, h * DN_HD:(h + 1) * DN_HD]) * (DN_HD ** -0.5) for h in heads]
        ks = [l2n(xc_ref[rows, w_dn + h * DN_HD:w_dn + (h + 1) * DN_HD]) for h in heads]
        vs = [xc_ref[rows, 2 * w_dn + h * DN_HD:2 * w_dn + (h + 1) * DN_HD] for h in heads]
        gcols = [gc_all[:, h:h + 1] for h in heads]
        betas = [beta_all[:, DN_HEADS + h:DN_HEADS + h + 1] for h in heads]
        decays = [jnp.where(tril, jnp.exp(jnp.minimum(gcols[h] - gc_t[h:h + 1, :], 0.0)), 0.0) for h in heads]
        k_betas = [ks[h] * betas[h] for h in heads]
        kqs = [_dot_nt(jnp.concatenate([k_betas[h], qs[h]], axis=0), ks[h]) for h in heads]
        lows = [jnp.where(strict, kqs[h][:c] * decays[h], 0.0) for h in heads]
        a_intras = [jnp.where(tril, kqs[h][c:] * decays[h], 0.0) for h in heads]
        egcs = [jnp.exp(gcols[h]) for h in heads]
        rhs = [jnp.concatenate([vs[h] * betas[h], k_betas[h] * egcs[h]], axis=1) for h in heads]
        q_egcs = [qs[h] * egcs[h] for h in heads]
        g_lasts = [gc_all[c - 1:c, h:h + 1] for h in heads]
        k_decs = [ks[h] * jnp.exp(g_lasts[h] - gcols[h]) for h in heads]
        e_lasts = [jnp.exp(g_lasts[h]) for h in heads]
        return lows, rhs, q_egcs, a_intras, k_decs, e_lasts

    n_chunks = ts // c
    parts = [local_part(ic) for ic in range(n_chunks)]
    t_invs = _unit_lower_inverses([low for p in parts for low in p[0]], masks, eye)
    uws = [_dot(t, r) for t, r in zip(t_invs, [r for p in parts for r in p[1]])]

    states = [state_ref[h] for h in heads]
    for ic in range(n_chunks):
        rows = slice(ic * c, (ic + 1) * c)
        _, _, q_egcs, a_intras, k_decs, e_lasts = parts[ic]
        uw = uws[ic * DN_HEADS:(ic + 1) * DN_HEADS]
        wss = [_dot(jnp.concatenate([uw[h][:, DN_HD:], q_egcs[h]], axis=0), states[h])
               for h in heads]
        v_news = [uw[h][:, :DN_HD] - wss[h][:c] for h in heads]
        states = [states[h] * e_lasts[h] + lax.dot_general(
            k_decs[h].astype(BF16), v_news[h].astype(BF16), (((0,), (0,)), ((), ())),
            preferred_element_type=F32) for h in heads]
        outs = [wss[h][c:] + _dot(a_intras[h], v_news[h]) for h in heads]
        for h in heads:
            cols = slice(h * DN_HD, (h + 1) * DN_HD)
            o_ref[rows, cols] = (_rms(outs[h], ng_ref[...])
                                 * _silu(z_ref[rows, cols].astype(F32))).astype(o_ref.dtype)
    for h in heads:
        state_ref[h] = states[h]


def _dn_call(proj, conv_w, a_log_pad, dt_pad, norm_g, batch, seq):
    m = proj.shape[0]
    ts = DN_TS
    ns = seq // ts
    w_dn = DN_HEADS * DN_HD
    return pl.pallas_call(
        functools.partial(_dn_kernel, ts=ts),
        grid=(batch, ns),
        in_specs=[pl.BlockSpec((ts, 3 * w_dn), lambda b, s: (b * ns + s, COL_DN_QKV // (3 * w_dn))),
                  pl.BlockSpec((ts, w_dn), lambda b, s: (b * ns + s, COL_DN_Z // w_dn)),
                  pl.BlockSpec((ts, LANES), lambda b, s: (b * ns + s, COL_AB // LANES)),
                  pl.BlockSpec((DN_CONV, 3 * w_dn), lambda b, s: (0, 0)),
                  pl.BlockSpec((1, LANES), lambda b, s: (0, 0)),
                  pl.BlockSpec((1, LANES), lambda b, s: (0, 0)),
                  pl.BlockSpec((1, DN_HD), lambda b, s: (0, 0))],
        out_specs=pl.BlockSpec((ts, w_dn), lambda b, s: (b * ns + s, 0)),
        out_shape=jax.ShapeDtypeStruct((m, w_dn), BRANCH_DTYPE),
        scratch_shapes=[pltpu.VMEM((ts + 8, 3 * w_dn), F32),
                        pltpu.VMEM((ts, 3 * w_dn), F32),
                        pltpu.VMEM((DN_HEADS, DN_HD, DN_HD), F32)],
        compiler_params=_params("arbitrary", "arbitrary"),
        name="deltanet",
    )(proj, proj, proj, conv_w, a_log_pad, dt_pad, norm_g)


def _rope_table_kernel(pos_ref, inv_ref, cos_ref, sin_ref):
    ang = pos_ref[0] * inv_ref[...]
    cos_ref[...] = jnp.cos(ang)
    sin_ref[...] = jnp.sin(ang)


def _rope_table_call(pos_f, inv_lane):
    batch, seq, _ = pos_f.shape
    ts = RT_TS
    ns = seq // ts
    out = jax.ShapeDtypeStruct((batch * seq, LANES), F32)
    return pl.pallas_call(
        _rope_table_kernel,
        grid=(batch, ns),
        in_specs=[pl.BlockSpec((1, ts, 1), lambda b, s: (b, s, 0)),
                  pl.BlockSpec((1, LANES), lambda b, s: (0, 0))],
        out_specs=[pl.BlockSpec((ts, LANES), lambda b, s: (b * ns + s, 0))] * 2,
        out_shape=[out, out],
        compiler_params=_params("arbitrary", "arbitrary"),
        name="rope_table",
    )(pos_f, inv_lane)


def _ret_kernel(q_ref, k_ref, v_ref, z_ref, cos_ref, sin_ref, g_ref, o_ref, state_ref, *, ts):
    c = RET_CHUNK
    w_qk = RET_HEADS * RET_QK_HD
    half = RET_QK_HD // 2

    @pl.when(pl.program_id(1) == 0)
    def _():
        state_ref[...] = jnp.zeros_like(state_ref)

    lane = lax.broadcasted_iota(jnp.int32, (1, w_qk), 1)
    first_half = (lane[:, :LANES] % RET_QK_HD) < half
    cos, sin = cos_ref[...], sin_ref[...]

    def rope(t):
        parts = []
        for l0 in range(0, w_qk, LANES):
            tl = t[:, l0:l0 + LANES]
            upper = pltpu.roll(tl, LANES - half, 1)
            lower = pltpu.roll(tl, half, 1)
            parts.append(tl * cos + jnp.where(first_half, -upper, lower) * sin)
        return jnp.concatenate(parts, axis=1)

    q = rope(q_ref[...].astype(F32)) * (RET_QK_HD ** -0.5)
    k = rope(k_ref[...].astype(F32))

    ri = lax.broadcasted_iota(jnp.int32, (c, c), 0)
    ci = lax.broadcasted_iota(jnp.int32, (c, c), 1)
    rel = (ri - ci).astype(F32)
    idx = lax.broadcasted_iota(jnp.int32, (c, 1), 0).astype(F32)

    heads = range(RET_HEADS)
    log_gammas = [math.log1p(-(2.0 ** (-5.0 - h))) for h in heads]
    intras = [jnp.where(rel >= 0.0, jnp.exp(jnp.maximum(rel, 0.0) * lg), 0.0) for lg in log_gammas]
    crosses = [jnp.exp((idx + 1.0) * lg) for lg in log_gammas]
    sdecs = [jnp.exp((c - 1.0 - idx) * lg) for lg in log_gammas]
    in_heads = [(lane >= h * RET_QK_HD) & (lane < (h + 1) * RET_QK_HD) for h in heads]
    states = [state_ref[h] for h in heads]
    for ic in range(ts // c):
        rows = slice(ic * c, (ic + 1) * c)
        kc = k[rows]
        kcb = kc.astype(BF16)
        qhs = [jnp.where(in_heads[h], q[rows], 0.0).astype(BF16) for h in heads]
        vhs = [v_ref[rows, h * RET_V_HD:(h + 1) * RET_V_HD] for h in heads]
        scores = [_dot_nt(qhs[h], kcb) * intras[h] for h in heads]
        o_cross = [_dot(qhs[h], states[h]) * crosses[h] for h in heads]
        kds = [(jnp.where(in_heads[h], kc, 0.0) * sdecs[h]).T for h in heads]
        states = [states[h] * math.exp(c * log_gammas[h]) + _dot(kds[h], vhs[h]) for h in heads]
        outs = [_dot(scores[h], vhs[h]) + o_cross[h] for h in heads]
        for h in heads:
            o = outs[h]
            mu = jnp.mean(o, axis=-1, keepdims=True)
            var = jnp.mean(jnp.square(o - mu), axis=-1, keepdims=True)
            y = (o - mu) * lax.rsqrt(var + EPS) * g_ref[:, h * RET_V_HD:(h + 1) * RET_V_HD]
            zh = z_ref[rows, h * RET_V_HD:(h + 1) * RET_V_HD].astype(F32)
            o_ref[rows, h * RET_V_HD:(h + 1) * RET_V_HD] = (y * _silu(zh)).astype(o_ref.dtype)
    for h in heads:
        state_ref[h] = states[h]


def _ret_call(proj, cos, sin, norm_g, batch, seq):
    m = proj.shape[0]
    ts = RT_TS
    ns = seq // ts
    w_qk = RET_HEADS * RET_QK_HD
    w_v = RET_HEADS * RET_V_HD
    return pl.pallas_call(
        functools.partial(_ret_kernel, ts=ts),
        grid=(batch, ns),
        in_specs=[pl.BlockSpec((ts, w_qk), lambda b, s: (b * ns + s, COL_RT_Q // w_qk)),
                  pl.BlockSpec((ts, w_qk), lambda b, s: (b * ns + s, COL_RT_K // w_qk)),
                  pl.BlockSpec((ts, w_v), lambda b, s: (b * ns + s, COL_RT_V // w_v)),
                  pl.BlockSpec((ts, w_v), lambda b, s: (b * ns + s, COL_RT_Z // w_v)),
                  pl.BlockSpec((ts, LANES), lambda b, s: (b * ns + s, 0)),
                  pl.BlockSpec((ts, LANES), lambda b, s: (b * ns + s, 0)),
                  pl.BlockSpec((1, w_v), lambda b, s: (0, 0))],
        out_specs=pl.BlockSpec((ts, w_v), lambda b, s: (b * ns + s, 0)),
        out_shape=jax.ShapeDtypeStruct((m, w_v), BRANCH_DTYPE),
        scratch_shapes=[pltpu.VMEM((RET_HEADS, w_qk, RET_V_HD), F32)],
        compiler_params=_params("arbitrary", "arbitrary"),
        name="retention",
    )(proj, proj, proj, proj, cos, sin, norm_g)


def _mem_kernel(q_ref, mem_ref, mg_ref, wkv_ref, o_ref, kv_ref):
    w_mem = MEM_HEADS * MEM_HD

    @pl.when(pl.program_id(1) == 0)
    def _():
        mem_n = _rms(mem_ref[0], mg_ref[...])
        kv_ref[...] = _dot(mem_n, wkv_ref[...])

    lane = lax.broadcasted_iota(jnp.int32, (1, w_mem), 1)
    q = q_ref[...].astype(F32) * (MEM_HD ** -0.5)
    km = kv_ref[:, 0:w_mem]
    vm = kv_ref[:, w_mem:2 * w_mem]
    heads = range(MEM_HEADS)
    in_heads = [(lane >= h * MEM_HD) & (lane < (h + 1) * MEM_HD) for h in heads]
    kmb, vmb = km.astype(BF16), vm.astype(BF16)
    scores = [_dot_nt(jnp.where(in_heads[h], q, 0.0), kmb) for h in heads]
    es = [jnp.exp(sc - jnp.max(sc, axis=-1, keepdims=True)) for sc in scores]
    ps = [e / jnp.sum(e, axis=-1, keepdims=True) for e in es]
    pvs = [_dot(p, vmb) for p in ps]
    out = pvs[0]
    for h in heads[1:]:
        out = jnp.where(in_heads[h], pvs[h], out)
    o_ref[...] = out.astype(o_ref.dtype)


def _mem_call(proj, mem, mem_g, w_kv, layer, batch, seq):
    m = proj.shape[0]
    ts = MEM_TS
    ns = seq // ts
    w_mem = MEM_HEADS * MEM_HD
    mem_len, d = mem.shape[1], mem.shape[2]
    return pl.pallas_call(
        _mem_kernel,
        grid=(batch, ns),
        in_specs=[pl.BlockSpec((ts, w_mem), lambda b, s: (b * ns + s, COL_MEM_Q // w_mem)),
                  pl.BlockSpec((1, mem_len, d), lambda b, s: (b, 0, 0)),
                  pl.BlockSpec((1, d), lambda b, s: (0, 0)),
                  pl.BlockSpec((None, d, 2 * w_mem), lambda b, s: (layer, 0, 0))],
        out_specs=pl.BlockSpec((ts, w_mem), lambda b, s: (b * ns + s, 0)),
        out_shape=jax.ShapeDtypeStruct((m, w_mem), BRANCH_DTYPE),
        scratch_shapes=[pltpu.VMEM((mem_len, 2 * w_mem), F32)],
        compiler_params=_params("arbitrary", "arbitrary"),
        name="mem_attn",
    )(proj, mem, mem_g, w_kv)


def _merge_kernel(x_ref, osb_ref, odn_ref, ort_ref, omem_ref, g_ref, wg_ref, bg_ref,
                  wsb_ref, wdn_ref, wrt_ref, wmem_ref, wout_ref, fg_ref, o_ref, *, final):
    d = x_ref.shape[1]
    x = x_ref[...]
    h = _rms(x, g_ref[...]).astype(BF16)
    merged = None
    branches = ((osb_ref, wsb_ref), (odn_ref, wdn_ref), (ort_ref, wrt_ref), (omem_ref, wmem_ref))
    for br, (ob_ref, wb_ref) in enumerate(branches):
        logits = jnp.dot(h, wg_ref[:, br * d:(br + 1) * d], preferred_element_type=F32)
        gate = jax.nn.sigmoid(logits + bg_ref[:, br * d:(br + 1) * d])
        term = gate * _dot(ob_ref[...], wb_ref[...])
        merged = term if merged is None else merged + term
    y = x + _dot(merged, wout_ref[...])
    if final:
        y = _rms(y, fg_ref[...])
    o_ref[...] = y


def _merge_call(xf, o_sb, o_dn, o_rt, o_mem, g, w_g, b_g, w_sb, w_dn, w_rt, w_mem, w_out, fg, layer, final):
    m, d = xf.shape
    tm = MERGE_TM

    def rows(width):
        return pl.BlockSpec((tm, width), lambda i: (i, 0))

    def of_layer(a):
        return pl.BlockSpec((None,) + a.shape[1:], lambda i: (layer, 0, 0), pipeline_mode=pl.Buffered(1))

    args = (xf, o_sb, o_dn, o_rt, o_mem, g, w_g, b_g, w_sb, w_dn, w_rt, w_mem, w_out, fg)
    in_specs = ([rows(a.shape[1]) for a in args[:5]] + [of_layer(a) for a in args[5:13]]
                + [pl.BlockSpec(fg.shape, lambda i: (0, 0), pipeline_mode=pl.Buffered(1))])
    return pl.pallas_call(
        functools.partial(_merge_kernel, final=final),
        grid=(m // tm,),
        in_specs=in_specs,
        out_specs=rows(d),
        out_shape=jax.ShapeDtypeStruct((m, d), F32),
        compiler_params=_params("arbitrary"),
        name="merge",
    )(*args)


def _arrange_w_in(w):
    sb = SB_HEADS * SB_HD
    dn = DN_HEADS * DN_HD
    widths = (4 * sb, 3 * dn, dn, 2 * DN_HEADS, 256, 256, 512, 512, 256)
    offs = [0]
    for wd in widths:
        offs.append(offs[-1] + wd)
    sb_qkvz, dn_qkv, dn_z, ab, rt_q, rt_k, rt_v, rt_z, mem_q = (
        w[..., offs[i]:offs[i + 1]].astype(BF16) for i in range(len(widths)))
    gates = w[..., offs[-1]:].astype(BF16)
    tail = jnp.zeros(w.shape[:-1] + (PROJ_COLS - COL_AB - 2 * DN_HEADS,), BF16)
    w_a = jnp.concatenate([dn_qkv, dn_z, sb_qkvz, rt_v, rt_z, rt_q, rt_k, mem_q, ab, tail], axis=-1)
    return w_a, gates


def kernel(x, mem, positions, norm_g, mem_norm_g, w_in, b_gate, dn_conv_w, dn_a_log, dn_dt_bias,
           dn_norm_g, ret_norm_g, w_mem_kv, w_br_sb, w_br_dn, w_br_ret, w_br_mem, w_out, final_norm_g):
    batch, seq, d = x.shape
    depth = w_in.shape[0]
    xf = x.reshape(batch * seq, d)
    pos_f = positions.astype(F32)[..., None]
    half = RET_QK_HD // 2
    inv = ROPE_BASE ** (-jnp.arange(half, dtype=F32) / half)
    inv_lane = jnp.tile(inv, LANES // half)[None, :]
    rope_cos, rope_sin = _rope_table_call(pos_f, inv_lane)
    lane_pad = ((0, 0), (0, LANES - DN_HEADS))
    a_log_pad = jnp.pad(dn_a_log, lane_pad)
    dt_pad = jnp.pad(dn_dt_bias, lane_pad)
    w_a, w_g = _arrange_w_in(w_in)
    w_kv, w_sb, w_dn, w_rt, w_mem, w_o = (t.astype(BF16) for t in (
        w_mem_kv, w_br_sb, w_br_dn, w_br_ret, w_br_mem, w_out))
    norm_g3, b_gate3 = norm_g[:, None, :], b_gate[:, None, :]
    for l in range(depth):
        proj = _proj_call(xf, norm_g3, w_a, l)
        o_sb = _sb_call(proj, batch, seq)
        o_dn = _dn_call(proj, dn_conv_w[l], a_log_pad[l][None, :], dt_pad[l][None, :],
                        dn_norm_g[l][None, :], batch, seq)
        o_rt = _ret_call(proj, rope_cos, rope_sin, ret_norm_g[l][None, :], batch, seq)
        o_mem = _mem_call(proj, mem, mem_norm_g[l][None, :], w_kv, l, batch, seq)
        xf = _merge_call(xf, o_sb, o_dn, o_rt, o_mem, norm_g3, w_g, b_gate3, w_sb, w_dn, w_rt, w_mem, w_o,
                         final_norm_g[None, :], l, final=(l == depth - 1))
    return xf.reshape(batch, seq, d)
```

```python
import functools
import math

import jax
import jax.numpy as jnp
from jax import lax
from jax.experimental import pallas as pl
from jax.experimental.pallas import tpu as pltpu

F32 = jnp.float32
BF16 = jnp.bfloat16
EPS = 1e-6

D_MODEL = 1024
SB_HEADS, SB_HD = 8, 64
DN_HEADS, DN_HD, DN_CONV, DN_CHUNK = 4, 128, 4, 128
RET_HEADS, RET_QK_HD, RET_V_HD, RET_CHUNK = 4, 64, 128, 128
MEM_HEADS, MEM_HD = 4, 64
N_BRANCH = 4
ROPE_BASE = 10000.0
LOG2E = 1.4426950408889634

LANES = 128
VMEM_LIMIT = 56 << 20

COL_DN_QKV = 0
COL_DN_Z = 1536
COL_SB_Q = 2048
COL_SB_K = 2560
COL_SB_V = 3072
COL_SB_Z = 3584
COL_RT_V = 4096
COL_RT_Z = 4608
COL_RT_Q = 5120
COL_RT_K = 5376
COL_MEM_Q = 5632
COL_AB = 5888
PROJ_COLS = 6144

PROJ_TM, PROJ_TN = 1024, 512
SB_TK = 256
DN_TS = 512
RT_TS = 512
MEM_TS = 512
MERGE_TM = 1024

PROJ_DTYPE = BF16
BRANCH_DTYPE = BF16


def _params(*sem):
    return pltpu.CompilerParams(dimension_semantics=sem, vmem_limit_bytes=VMEM_LIMIT)


def _rms(x, g):
    return x * lax.rsqrt(jnp.mean(x * x, axis=-1, keepdims=True) + EPS) * g


def _silu(x):
    half = 0.5 * x
    return half + half * jnp.tanh(half)


def _dot(a, b):
    return jnp.dot(a.astype(BF16), b.astype(BF16), preferred_element_type=F32)


def _dot_nt(a, b):
    return lax.dot_general(a.astype(BF16), b.astype(BF16), (((1,), (1,)), ((), ())),
                           preferred_element_type=F32)


def _proj_kernel(x_ref, g_ref, w_ref, o_ref):
    h = _rms(x_ref[...], g_ref[...]).astype(BF16)
    n = w_ref.shape[1]
    for c0 in range(0, n, PROJ_TN):
        o_ref[:, c0:c0 + PROJ_TN] = jnp.dot(h, w_ref[:, c0:c0 + PROJ_TN],
                                            preferred_element_type=F32).astype(o_ref.dtype)


def _proj_call(xf, g, w_a, layer):
    m, d = xf.shape
    n = w_a.shape[2]
    return pl.pallas_call(
        _proj_kernel,
        grid=(m // PROJ_TM,),
        in_specs=[pl.BlockSpec((PROJ_TM, d), lambda i: (i, 0)),
                  pl.BlockSpec((None, 1, d), lambda i: (layer, 0, 0), pipeline_mode=pl.Buffered(1)),
                  pl.BlockSpec((None, d, n), lambda i: (layer, 0, 0), pipeline_mode=pl.Buffered(1))],
        out_specs=pl.BlockSpec((PROJ_TM, n), lambda i: (i, 0)),
        out_shape=jax.ShapeDtypeStruct((m, n), PROJ_DTYPE),
        compiler_params=_params("arbitrary"),
        name="proj",
    )(xf, g, w_a)


def _sb_kernel(q_ref, k_ref, v_ref, z_ref, o_ref, *, tk):
    seq = q_ref.shape[0]
    nb = seq // tk
    lane = lax.broadcasted_iota(jnp.int32, (1, LANES), 1)
    row = lax.broadcasted_iota(jnp.int32, (tk, tk), 0)
    col = lax.broadcasted_iota(jnp.int32, (tk, tk), 1)
    tri = (row >= col).astype(BF16)
    diag_mask = col < row
    head_lanes = [(lane >= hh * SB_HD) & (lane < (hh + 1) * SB_HD) for hh in range(2)]

    def blk(r):
        return slice(r * tk, (r + 1) * tk)

    qms = []
    for r in range(nb):
        q = q_ref[blk(r), :].astype(F32) * (SB_HD ** -0.5 * LOG2E)
        qms.append([jnp.where(head_lanes[hh], q, 0.0).astype(BF16) for hh in range(2)])
    zero_c = jnp.zeros((tk, 1), F32)
    carry = {(r, hh): zero_c for r in range(nb) for hh in range(2)}
    acc = [jnp.zeros((tk, LANES), F32) for _ in range(nb)]

    def chains(c):
        return [(r, hh) for r in range(c, nb) for hh in range(2)]

    def scores(c):
        kb = k_ref[blk(c), :].astype(BF16)
        z2s = {ch: _dot_nt(qms[ch[0]][ch[1]], kb) for ch in chains(c)}
        sp2s = {}
        for ch, z2 in z2s.items():
            sp2 = jnp.maximum(z2, 0.0) + jnp.log2(1.0 + jnp.exp2(-jnp.abs(z2)))
            sp2s[ch] = (sp2 if ch[0] != c else jnp.where(diag_mask, sp2, 0.0)).astype(BF16)
        return z2s, sp2s

    def weights(c, z2s, sp2s):
        incls = {ch: jnp.dot(sp2s[ch], tri, preferred_element_type=F32) for ch in chains(c)}
        ws = {}
        for ch in chains(c):
            w = jnp.exp2(z2s[ch] - incls[ch] - carry[ch])
            ws[ch] = (w if ch[0] != c else jnp.where(diag_mask, w, 0.0)).astype(BF16)
            carry[ch] = carry[ch] + incls[ch][:, 0:1]
        return ws

    def accumulate(c, ws):
        vb = v_ref[blk(c), :]
        vh = [jnp.where(head_lanes[hh], vb, jnp.zeros_like(vb)).astype(BF16) for hh in range(2)]
        for r in range(c, nb):
            acc[r] = (acc[r] + jnp.dot(ws[(r, 0)], vh[0], preferred_element_type=F32)
                      + jnp.dot(ws[(r, 1)], vh[1], preferred_element_type=F32))

    pending = scores(nb - 1)
    for c in range(nb - 1, -1, -1):
        ahead = scores(c - 1) if c > 0 else None
        accumulate(c, weights(c, *pending))
        pending = ahead
    for r in range(nb):
        o_ref[blk(r), :] = (acc[r] * _silu(z_ref[blk(r), :].astype(F32))).astype(o_ref.dtype)


def _sb_call(proj, batch, seq):
    m = proj.shape[0]
    pairs = SB_HEADS * SB_HD // LANES
    cq, ck, cv, cz = (c // LANES for c in (COL_SB_Q, COL_SB_K, COL_SB_V, COL_SB_Z))
    return pl.pallas_call(
        functools.partial(_sb_kernel, tk=SB_TK),
        grid=(batch, pairs),
        in_specs=[pl.BlockSpec((seq, LANES), lambda b, p: (b, cq + p)),
                  pl.BlockSpec((seq, LANES), lambda b, p: (b, ck + p)),
                  pl.BlockSpec((seq, LANES), lambda b, p: (b, cv + p)),
                  pl.BlockSpec((seq, LANES), lambda b, p: (b, cz + p))],
        out_specs=pl.BlockSpec((seq, LANES), lambda b, p: (b, p)),
        out_shape=jax.ShapeDtypeStruct((m, SB_HEADS * SB_HD), BRANCH_DTYPE),
        compiler_params=_params("arbitrary", "arbitrary"),
        name="sb_attn",
    )(proj, proj, proj, proj)


def _level_masks(c):
    ri = lax.broadcasted_iota(jnp.int32, (c, c), 0)
    ci = lax.broadcasted_iota(jnp.int32, (c, c), 1)
    masks = [((ri == ci + 1) & ((ri & 1) == 1)).astype(F32)]
    b = 2
    while b < c:
        shift = int(math.log2(2 * b))
        masks.append((((ri >> shift) == (ci >> shift)) & ((ri & b) != 0) & ((ci & b) == 0)).astype(F32))
        b *= 2
    return masks


def _unit_lower_inverses(lows, masks, eye):
    invs = [eye - low * masks[0] for low in lows]
    for m in masks[1:]:
        inner = [_dot(low * m, inv) for low, inv in zip(lows, invs)]
        invs = [inv - _dot(inv, x) for inv, x in zip(invs, inner)]
    return invs


def _dn_kernel(qkv_ref, z_ref, ab_ref, cw_ref, alog_ref, dt_ref, ng_ref, o_ref,
               ext_ref, xc_ref, state_ref, *, ts):
    c = DN_CHUNK
    w_dn = DN_HEADS * DN_HD
    pad = 8

    @pl.when(pl.program_id(1) == 0)
    def _():
        ext_ref[0:pad, :] = jnp.zeros((pad, 3 * w_dn), F32)
        state_ref[...] = jnp.zeros_like(state_ref)

    @pl.when(pl.program_id(1) > 0)
    def _():
        ext_ref[0:pad, :] = ext_ref[ts:ts + pad, :]

    ext_ref[pad:pad + ts, :] = qkv_ref[...].astype(F32)

    for col0 in range(0, 3 * w_dn, LANES):
        acc = None
        for j in range(DN_CONV):
            lo = pad - (DN_CONV - 1) + j
            term = ext_ref[lo:lo + ts, col0:col0 + LANES] * cw_ref[j:j + 1, col0:col0 + LANES]
            acc = term if acc is None else acc + term
        xc_ref[:, col0:col0 + LANES] = _silu(acc)

    ri = lax.broadcasted_iota(jnp.int32, (c, c), 0)
    ci = lax.broadcasted_iota(jnp.int32, (c, c), 1)
    tril = ri >= ci
    strict = ri > ci
    tril_b = tril.astype(BF16)

    def l2n(t):
        return t * lax.rsqrt(jnp.sum(t * t, axis=-1, keepdims=True) + EPS)

    masks = _level_masks(c)
    eye = (ri == ci).astype(F32)
    heads = range(DN_HEADS)

    def local_part(ic):
        rows = slice(ic * c, (ic + 1) * c)
        ab = ab_ref[rows, :].astype(F32)
        pre = ab + dt_ref[...]
        sp = jnp.maximum(pre, 0.0) + jnp.log(1.0 + jnp.exp(-jnp.abs(pre)))
        g_all = -jnp.exp(alog_ref[...]) * sp
        beta_all = jax.nn.sigmoid(ab)
        g_hi = g_all.astype(BF16)
        g_lo = (g_all - g_hi.astype(F32)).astype(BF16)
        gc_all = (jnp.dot(tril_b, g_hi, preferred_element_type=F32)
                  + jnp.dot(tril_b, g_lo, preferred_element_type=F32))
        gc_t = gc_all.T
        qs = [l2n(xc_ref[rows, h * DN_HD:(h + 1) * DN_HD]) * (DN_HD ** -0.5) for h in heads]
        ks = [l2n(xc_ref[rows, w_dn + h * DN_HD:w_dn + (h + 1) * DN_HD]) for h in heads]
        vs = [xc_ref[rows, 2 * w_dn + h * DN_HD:2 * w_dn + (h + 1) * DN_HD] for h in heads]
        gcols = [gc_all[:, h:h + 1] for h in heads]
        betas = [beta_all[:, DN_HEADS + h:DN_HEADS + h + 1] for h in heads]
        decays = [jnp.where(tril, jnp.exp(jnp.minimum(gcols[h] - gc_t[h:h + 1, :], 0.0)), 0.0) for h in heads]
        k_betas = [ks[h] * betas[h] for h in heads]
        kqs = [_dot_nt(jnp.concatenate([k_betas[h], qs[h]], axis=0), ks[h]) for h in heads]
        lows = [jnp.where(strict, kqs[h][:c] * decays[h], 0.0) for h in heads]
        a_intras = [jnp.where(tril, kqs[h][c:] * decays[h], 0.0) for h in heads]
        egcs = [jnp.exp(gcols[h]) for h in heads]
        rhs = [jnp.concatenate([vs[h] * betas[h], k_betas[h] * egcs[h]], axis=1) for h in heads]
        q_egcs = [qs[h] * egcs[h] for h in heads]
        g_lasts = [gc_all[c - 1:c, h:h + 1] for h in heads]
        k_decs = [ks[h] * jnp.exp(g_lasts[h] - gcols[h]) for h in heads]
        e_lasts = [jnp.exp(g_lasts[h]) for h in heads]
        return lows, rhs, q_egcs, a_intras, k_decs, e_lasts

    n_chunks = ts // c
    parts = [local_part(ic) for ic in range(n_chunks)]
    t_invs = _unit_lower_inverses([low for p in parts for low in p[0]], masks, eye)
    uws = [_dot(t, r) for t, r in zip(t_invs, [r for p in parts for r in p[1]])]

    states = [state_ref[h] for h in heads]
    for ic in range(n_chunks):
        rows = slice(ic * c, (ic + 1) * c)
        _, _, q_egcs, a_intras, k_decs, e_lasts = parts[ic]
        uw = uws[ic * DN_HEADS:(ic + 1) * DN_HEADS]
        wss = [_dot(jnp.concatenate([uw[h][:, DN_HD:], q_egcs[h]], axis=0), states[h])
               for h in heads]
        v_news = [uw[h][:, :DN_HD] - wss[h][:c] for h in heads]
        states = [states[h] * e_lasts[h] + lax.dot_general(
            k_decs[h].astype(BF16), v_news[h].astype(BF16), (((0,), (0,)), ((), ())),
            preferred_element_type=F32) for h in heads]
        outs = [wss[h][c:] + _dot(a_intras[h], v_news[h]) for h in heads]
        for h in heads:
            cols = slice(h * DN_HD, (h + 1) * DN_HD)
            o_ref[rows, cols] = (_rms(outs[h], ng_ref[...])
                                 * _silu(z_ref[rows, cols].astype(F32))).astype(o_ref.dtype)
    for h in heads:
        state_ref[h] = states[h]


def _dn_call(proj, conv_w, a_log_pad, dt_pad, norm_g, batch, seq):
    m = proj.shape[0]
    ts = DN_TS
    ns = seq // ts
    w_dn = DN_HEADS * DN_HD
    return pl.pallas_call(
        functools.partial(_dn_kernel, ts=ts),
        grid=(batch, ns),
        in_specs=[pl.BlockSpec((ts, 3 * w_dn), lambda b, s: (b * ns + s, COL_DN_QKV // (3 * w_dn))),
                  pl.BlockSpec((ts, w_dn), lambda b, s: (b * ns + s, COL_DN_Z // w_dn)),
                  pl.BlockSpec((ts, LANES), lambda b, s: (b * ns + s, COL_AB // LANES)),
                  pl.BlockSpec((DN_CONV, 3 * w_dn), lambda b, s: (0, 0)),
                  pl.BlockSpec((1, LANES), lambda b, s: (0, 0)),
                  pl.BlockSpec((1, LANES), lambda b, s: (0, 0)),
                  pl.BlockSpec((1, DN_HD), lambda b, s: (0, 0))],
        out_specs=pl.BlockSpec((ts, w_dn), lambda b, s: (b * ns + s, 0)),
        out_shape=jax.ShapeDtypeStruct((m, w_dn), BRANCH_DTYPE),
        scratch_shapes=[pltpu.VMEM((ts + 8, 3 * w_dn), F32),
                        pltpu.VMEM((ts, 3 * w_dn), F32),
                        pltpu.VMEM((DN_HEADS, DN_HD, DN_HD), F32)],
        compiler_params=_params("arbitrary", "arbitrary"),
        name="deltanet",
    )(proj, proj, proj, conv_w, a_log_pad, dt_pad, norm_g)


def _rope_table_kernel(pos_ref, inv_ref, cos_ref, sin_ref):
    ang = pos_ref[0] * inv_ref[...]
    cos_ref[...] = jnp.cos(ang)
    sin_ref[...] = jnp.sin(ang)


def _rope_table_call(pos_f, inv_lane):
    batch, seq, _ = pos_f.shape
    ts = RT_TS
    ns = seq // ts
    out = jax.ShapeDtypeStruct((batch * seq, LANES), F32)
    return pl.pallas_call(
        _rope_table_kernel,
        grid=(batch, ns),
        in_specs=[pl.BlockSpec((1, ts, 1), lambda b, s: (b, s, 0)),
                  pl.BlockSpec((1, LANES), lambda b, s: (0, 0))],
        out_specs=[pl.BlockSpec((ts, LANES), lambda b, s: (b * ns + s, 0))] * 2,
        out_shape=[out, out],
        compiler_params=_params("arbitrary", "arbitrary"),
        name="rope_table",
    )(pos_f, inv_lane)


def _ret_kernel(q_ref, k_ref, v_ref, z_ref, cos_ref, sin_ref, g_ref, o_ref, state_ref, *, ts):
    c = RET_CHUNK
    w_qk = RET_HEADS * RET_QK_HD
    half = RET_QK_HD // 2

    @pl.when(pl.program_id(1) == 0)
    def _():
        state_ref[...] = jnp.zeros_like(state_ref)

    lane = lax.broadcasted_iota(jnp.int32, (1, w_qk), 1)
    first_half = (lane[:, :LANES] % RET_QK_HD) < half
    cos, sin = cos_ref[...], sin_ref[...]

    def rope(t):
        parts = []
        for l0 in range(0, w_qk, LANES):
            tl = t[:, l0:l0 + LANES]
            upper = pltpu.roll(tl, LANES - half, 1)
            lower = pltpu.roll(tl, half, 1)
            parts.append(tl * cos + jnp.where(first_half, -upper, lower) * sin)
        return jnp.concatenate(parts, axis=1)

    q = rope(q_ref[...].astype(F32)) * (RET_QK_HD ** -0.5)
    k = rope(k_ref[...].astype(F32))

    ri = lax.broadcasted_iota(jnp.int32, (c, c), 0)
    ci = lax.broadcasted_iota(jnp.int32, (c, c), 1)
    rel = (ri - ci).astype(F32)
    idx = lax.broadcasted_iota(jnp.int32, (c, 1), 0).astype(F32)

    heads = range(RET_HEADS)
    log_gammas = [math.log1p(-(2.0 ** (-5.0 - h))) for h in heads]
    intras = [jnp.where(rel >= 0.0, jnp.exp(jnp.maximum(rel, 0.0) * lg), 0.0) for lg in log_gammas]
    crosses = [jnp.exp((idx + 1.0) * lg) for lg in log_gammas]
    sdecs = [jnp.exp((c - 1.0 - idx) * lg) for lg in log_gammas]
    in_heads = [(lane >= h * RET_QK_HD) & (lane < (h + 1) * RET_QK_HD) for h in heads]
    states = [state_ref[h] for h in heads]
    for ic in range(ts // c):
        rows = slice(ic * c, (ic + 1) * c)
        kc = k[rows]
        kcb = kc.astype(BF16)
        qhs = [jnp.where(in_heads[h], q[rows], 0.0).astype(BF16) for h in heads]
        vhs = [v_ref[rows, h * RET_V_HD:(h + 1) * RET_V_HD] for h in heads]
        scores = [_dot_nt(qhs[h], kcb) * intras[h] for h in heads]
        o_cross = [_dot(qhs[h], states[h]) * crosses[h] for h in heads]
        kds = [(jnp.where(in_heads[h], kc, 0.0) * sdecs[h]).T for h in heads]
        states = [states[h] * math.exp(c * log_gammas[h]) + _dot(kds[h], vhs[h]) for h in heads]
        outs = [_dot(scores[h], vhs[h]) + o_cross[h] for h in heads]
        for h in heads:
            o = outs[h]
            mu = jnp.mean(o, axis=-1, keepdims=True)
            var = jnp.mean(jnp.square(o - mu), axis=-1, keepdims=True)
            y = (o - mu) * lax.rsqrt(var + EPS) * g_ref[:, h * RET_V_HD:(h + 1) * RET_V_HD]
            zh = z_ref[rows, h * RET_V_HD:(h + 1) * RET_V_HD].astype(F32)
            o_ref[rows, h * RET_V_HD:(h + 1) * RET_V_HD] = (y * _silu(zh)).astype(o_ref.dtype)
    for h in heads:
        state_ref[h] = states[h]


def _ret_call(proj, cos, sin, norm_g, batch, seq):
    m = proj.shape[0]
    ts = RT_TS
    ns = seq // ts
    w_qk = RET_HEADS * RET_QK_HD
    w_v = RET_HEADS * RET_V_HD
    return pl.pallas_call(
        functools.partial(_ret_kernel, ts=ts),
        grid=(batch, ns),
        in_specs=[pl.BlockSpec((ts, w_qk), lambda b, s: (b * ns + s, COL_RT_Q // w_qk)),
                  pl.BlockSpec((ts, w_qk), lambda b, s: (b * ns + s, COL_RT_K // w_qk)),
                  pl.BlockSpec((ts, w_v), lambda b, s: (b * ns + s, COL_RT_V // w_v)),
                  pl.BlockSpec((ts, w_v), lambda b, s: (b * ns + s, COL_RT_Z // w_v)),
                  pl.BlockSpec((ts, LANES), lambda b, s: (b * ns + s, 0)),
                  pl.BlockSpec((ts, LANES), lambda b, s: (b * ns + s, 0)),
                  pl.BlockSpec((1, w_v), lambda b, s: (0, 0))],
        out_specs=pl.BlockSpec((ts, w_v), lambda b, s: (b * ns + s, 0)),
        out_shape=jax.ShapeDtypeStruct((m, w_v), BRANCH_DTYPE),
        scratch_shapes=[pltpu.VMEM((RET_HEADS, w_qk, RET_V_HD), F32)],
        compiler_params=_params("arbitrary", "arbitrary"),
        name="retention",
    )(proj, proj, proj, proj, cos, sin, norm_g)


def _mem_kernel(q_ref, mem_ref, mg_ref, wkv_ref, o_ref, kv_ref):
    w_mem = MEM_HEADS * MEM_HD

    @pl.when(pl.program_id(1) == 0)
    def _():
        mem_n = _rms(mem_ref[0], mg_ref[...])
        kv_ref[...] = _dot(mem_n, wkv_ref[...])

    lane = lax.broadcasted_iota(jnp.int32, (1, w_mem), 1)
    q = q_ref[...].astype(F32) * (MEM_HD ** -0.5)
    km = kv_ref[:, 0:w_mem]
    vm = kv_ref[:, w_mem:2 * w_mem]
    heads = range(MEM_HEADS)
    in_heads = [(lane >= h * MEM_HD) & (lane < (h + 1) * MEM_HD) for h in heads]
    kmb, vmb = km.astype(BF16), vm.astype(BF16)
    scores = [_dot_nt(jnp.where(in_heads[h], q, 0.0), kmb) for h in heads]
    es = [jnp.exp(sc - jnp.max(sc, axis=-1, keepdims=True)) for sc in scores]
    ps = [e / jnp.sum(e, axis=-1, keepdims=True) for e in es]
    pvs = [_dot(p, vmb) for p in ps]
    out = pvs[0]
    for h in heads[1:]:
        out = jnp.where(in_heads[h], pvs[h], out)
    o_ref[...] = out.astype(o_ref.dtype)


def _mem_call(proj, mem, mem_g, w_kv, layer, batch, seq):
    m = proj.shape[0]
    ts = MEM_TS
    ns = seq // ts
    w_mem = MEM_HEADS * MEM_HD
    mem_len, d = mem.shape[1], mem.shape[2]
    return pl.pallas_call(
        _mem_kernel,
        grid=(batch, ns),
        in_specs=[pl.BlockSpec((ts, w_mem), lambda b, s: (b * ns + s, COL_MEM_Q // w_mem)),
                  pl.BlockSpec((1, mem_len, d), lambda b, s: (b, 0, 0)),
                  pl.BlockSpec((1, d), lambda b, s: (0, 0)),
                  pl.BlockSpec((None, d, 2 * w_mem), lambda b, s: (layer, 0, 0))],
        out_specs=pl.BlockSpec((ts, w_mem), lambda b, s: (b * ns + s, 0)),
        out_shape=jax.ShapeDtypeStruct((m, w_mem), BRANCH_DTYPE),
        scratch_shapes=[pltpu.VMEM((mem_len, 2 * w_mem), F32)],
        compiler_params=_params("arbitrary", "arbitrary"),
        name="mem_attn",
    )(proj, mem, mem_g, w_kv)


def _merge_kernel(x_ref, osb_ref, odn_ref, ort_ref, omem_ref, g_ref, wg_ref, bg_ref,
                  wsb_ref, wdn_ref, wrt_ref, wmem_ref, wout_ref, fg_ref, o_ref, *, final):
    d = x_ref.shape[1]
    x = x_ref[...]
    h = _rms(x, g_ref[...]).astype(BF16)
    merged = None
    branches = ((osb_ref, wsb_ref), (odn_ref, wdn_ref), (ort_ref, wrt_ref), (omem_ref, wmem_ref))
    for br, (ob_ref, wb_ref) in enumerate(branches):
        logits = jnp.dot(h, wg_ref[:, br * d:(br + 1) * d], preferred_element_type=F32)
        gate = jax.nn.sigmoid(logits + bg_ref[:, br * d:(br + 1) * d])
        term = gate * _dot(ob_ref[...], wb_ref[...])
        merged = term if merged is None else merged + term
    y = x + _dot(merged, wout_ref[...])
    if final:
        y = _rms(y, fg_ref[...])
    o_ref[...] = y


def _merge_call(xf, o_sb, o_dn, o_rt, o_mem, g, w_g, b_g, w_sb, w_dn, w_rt, w_mem, w_out, fg, layer, final):
    m, d = xf.shape
    tm = MERGE_TM

    def rows(width):
        return pl.BlockSpec((tm, width), lambda i: (i, 0))

    def of_layer(a):
        return pl.BlockSpec((None,) + a.shape[1:], lambda i: (layer, 0, 0), pipeline_mode=pl.Buffered(1))

    args = (xf, o_sb, o_dn, o_rt, o_mem, g, w_g, b_g, w_sb, w_dn, w_rt, w_mem, w_out, fg)
    in_specs = ([rows(a.shape[1]) for a in args[:5]] + [of_layer(a) for a in args[5:13]]
                + [pl.BlockSpec(fg.shape, lambda i: (0, 0), pipeline_mode=pl.Buffered(1))])
    return pl.pallas_call(
        functools.partial(_merge_kernel, final=final),
        grid=(m // tm,),
        in_specs=in_specs,
        out_specs=rows(d),
        out_shape=jax.ShapeDtypeStruct((m, d), F32),
        compiler_params=_params("arbitrary"),
        name="merge",
    )(*args)


def _arrange_kernel(w_ref, wa_ref, wg_ref):
    sb = SB_HEADS * SB_HD
    dn = DN_HEADS * DN_HD
    n_ab = 2 * DN_HEADS
    segments = ((0, 4 * sb, COL_SB_Q), (4 * sb, 3 * dn, COL_DN_QKV), (4 * sb + 3 * dn, dn, COL_DN_Z))
    src = 4 * sb + 4 * dn + n_ab
    for width, dst in ((256, COL_RT_Q), (256, COL_RT_K), (512, COL_RT_V), (512, COL_RT_Z), (256, COL_MEM_Q)):
        segments += ((src, width, dst),)
        src += width
    for s0, width, d0 in segments:
        wa_ref[:, d0:d0 + width] = w_ref[:, s0:s0 + width].astype(wa_ref.dtype)
    wa_ref[:, COL_AB:] = jnp.zeros((wa_ref.shape[0], wa_ref.shape[1] - COL_AB), wa_ref.dtype)
    wa_ref[:, COL_AB:COL_AB + n_ab] = w_ref[:, 4 * sb + 4 * dn:4 * sb + 4 * dn + n_ab].astype(wa_ref.dtype)
    wg_ref[...] = w_ref[:, src:].astype(wg_ref.dtype)


def _arrange_w_in(w):
    depth, d, n_in = w.shape
    n_gate = N_BRANCH * d
    tr = 256
    w_a, w_g = pl.pallas_call(
        _arrange_kernel,
        grid=(depth * d // tr,),
        in_specs=[pl.BlockSpec((tr, n_in), lambda i: (i, 0))],
        out_specs=[pl.BlockSpec((tr, PROJ_COLS), lambda i: (i, 0)),
                   pl.BlockSpec((tr, n_gate), lambda i: (i, 0))],
        out_shape=[jax.ShapeDtypeStruct((depth * d, PROJ_COLS), BF16),
                   jax.ShapeDtypeStruct((depth * d, n_gate), BF16)],
        compiler_params=_params("arbitrary"),
        name="arrange_w_in",
    )(w.reshape(depth * d, n_in))
    return w_a.reshape(depth, d, PROJ_COLS), w_g.reshape(depth, d, n_gate)


def kernel(x, mem, positions, norm_g, mem_norm_g, w_in, b_gate, dn_conv_w, dn_a_log, dn_dt_bias,
           dn_norm_g, ret_norm_g, w_mem_kv, w_br_sb, w_br_dn, w_br_ret, w_br_mem, w_out, final_norm_g):
    batch, seq, d = x.shape
    depth = w_in.shape[0]
    xf = x.reshape(batch * seq, d)
    pos_f = positions.astype(F32)[..., None]
    half = RET_QK_HD // 2
    inv = ROPE_BASE ** (-jnp.arange(half, dtype=F32) / half)
    inv_lane = jnp.tile(inv, LANES // half)[None, :]
    rope_cos, rope_sin = _rope_table_call(pos_f, inv_lane)
    lane_pad = ((0, 0), (0, LANES - DN_HEADS))
    a_log_pad = jnp.pad(dn_a_log, lane_pad)
    dt_pad = jnp.pad(dn_dt_bias, lane_pad)
    w_a, w_g = _arrange_w_in(w_in)
    w_kv, w_sb, w_dn, w_rt, w_mem, w_o = (t.astype(BF16) for t in (
        w_mem_kv, w_br_sb, w_br_dn, w_br_ret, w_br_mem, w_out))
    norm_g3, b_gate3 = norm_g[:, None, :], b_gate[:, None, :]
    for l in range(depth):
        proj = _proj_call(xf, norm_g3, w_a, l)
        o_sb = _sb_call(proj, batch, seq)
        o_dn = _dn_call(proj, dn_conv_w[l], a_log_pad[l][None, :], dt_pad[l][None, :],
                        dn_norm_g[l][None, :], batch, seq)
        o_rt = _ret_call(proj, rope_cos, rope_sin, ret_norm_g[l][None, :], batch, seq)
        o_mem = _mem_call(proj, mem, mem_norm_g[l][None, :], w_kv, l, batch, seq)
        xf = _merge_call(xf, o_sb, o_dn, o_rt, o_mem, norm_g3, w_g, b_gate3, w_sb, w_dn, w_rt, w_mem, w_o,
                         final_norm_g[None, :], l, final=(l == depth - 1))
    return xf.reshape(batch, seq, d)
```

```python
import functools
import math

import jax
import jax.numpy as jnp
from jax import lax
from jax.experimental import pallas as pl
from jax.experimental.pallas import tpu as pltpu

F32 = jnp.float32
BF16 = jnp.bfloat16
EPS = 1e-6

D_MODEL = 1024
SB_HEADS, SB_HD = 8, 64
DN_HEADS, DN_HD, DN_CONV, DN_CHUNK = 4, 128, 4, 128
RET_HEADS, RET_QK_HD, RET_V_HD, RET_CHUNK = 4, 64, 128, 128
MEM_HEADS, MEM_HD = 4, 64
N_BRANCH = 4
ROPE_BASE = 10000.0
LOG2E = 1.4426950408889634

LANES = 128
VMEM_LIMIT = 56 << 20

COL_DN_QKV = 0
COL_DN_Z = 1536
COL_SB_Q = 2048
COL_SB_K = 2560
COL_SB_V = 3072
COL_SB_Z = 3584
COL_RT_V = 4096
COL_RT_Z = 4608
COL_RT_Q = 5120
COL_RT_K = 5376
COL_MEM_Q = 5632
COL_AB = 5888
PROJ_COLS = 6144

PROJ_TM, PROJ_TN = 1024, 512
SB_TK = 256
DN_TS = 512
RT_TS = 512
MEM_TS = 512
MERGE_TM = 1024

PROJ_DTYPE = BF16
BRANCH_DTYPE = BF16


def _params(*sem):
    return pltpu.CompilerParams(dimension_semantics=sem, vmem_limit_bytes=VMEM_LIMIT)


def _rms(x, g):
    return x * lax.rsqrt(jnp.mean(x * x, axis=-1, keepdims=True) + EPS) * g


def _silu(x):
    half = 0.5 * x
    return half + half * jnp.tanh(half)


def _dot(a, b):
    return jnp.dot(a.astype(BF16), b.astype(BF16), preferred_element_type=F32)


def _dot_nt(a, b):
    return lax.dot_general(a.astype(BF16), b.astype(BF16), (((1,), (1,)), ((), ())),
                           preferred_element_type=F32)


def _proj_kernel(x_ref, g_ref, w_ref, o_ref, h_ref):
    h = _rms(x_ref[...], g_ref[...]).astype(BF16)
    h_ref[...] = h
    n = w_ref.shape[0]
    for c0 in range(0, n, PROJ_TN):
        o_ref[:, c0:c0 + PROJ_TN] = _dot_nt(h, w_ref[c0:c0 + PROJ_TN, :]).astype(o_ref.dtype)


def _proj_call(xf, g, w_a, layer):
    m, d = xf.shape
    n = w_a.shape[1]
    return pl.pallas_call(
        _proj_kernel,
        grid=(m // PROJ_TM,),
        in_specs=[pl.BlockSpec((PROJ_TM, d), lambda i: (i, 0)),
                  pl.BlockSpec((None, 1, d), lambda i: (layer, 0, 0), pipeline_mode=pl.Buffered(1)),
                  pl.BlockSpec((None, n, d), lambda i: (layer, 0, 0), pipeline_mode=pl.Buffered(1))],
        out_specs=[pl.BlockSpec((PROJ_TM, n), lambda i: (i, 0)),
                   pl.BlockSpec((PROJ_TM, d), lambda i: (i, 0))],
        out_shape=[jax.ShapeDtypeStruct((m, n), PROJ_DTYPE),
                   jax.ShapeDtypeStruct((m, d), BF16)],
        compiler_params=_params("arbitrary"),
        name="proj",
    )(xf, g, w_a)


def _sb_kernel(q_ref, k_ref, v_ref, z_ref, h_ref, wg_ref, o_ref, lg_ref, *, tk):
    seq = q_ref.shape[0]
    nb = seq // tk
    lane = lax.broadcasted_iota(jnp.int32, (1, LANES), 1)
    row = lax.broadcasted_iota(jnp.int32, (tk, tk), 0)
    col = lax.broadcasted_iota(jnp.int32, (tk, tk), 1)
    tri = (row >= col).astype(BF16)
    diag_mask = col < row
    head_lanes = [(lane >= hh * SB_HD) & (lane < (hh + 1) * SB_HD) for hh in range(2)]

    def blk(r):
        return slice(r * tk, (r + 1) * tk)

    qms = []
    for r in range(nb):
        q = q_ref[blk(r), :].astype(F32) * (SB_HD ** -0.5 * LOG2E)
        qms.append([jnp.where(head_lanes[hh], q, 0.0).astype(BF16) for hh in range(2)])
    zero_c = jnp.zeros((tk, 1), F32)
    carry = {(r, hh): zero_c for r in range(nb) for hh in range(2)}
    acc = [jnp.zeros((tk, LANES), F32) for _ in range(nb)]

    def chains(c):
        return [(r, hh) for r in range(c, nb) for hh in range(2)]

    def scores(c):
        kb = k_ref[blk(c), :].astype(BF16)
        z2s = {ch: _dot_nt(qms[ch[0]][ch[1]], kb) for ch in chains(c)}
        sp2s = {}
        for ch, z2 in z2s.items():
            sp2 = jnp.maximum(z2, 0.0) + jnp.log2(1.0 + jnp.exp2(-jnp.abs(z2)))
            sp2s[ch] = (sp2 if ch[0] != c else jnp.where(diag_mask, sp2, 0.0)).astype(BF16)
        return z2s, sp2s

    def weights(c, z2s, sp2s):
        incls = {ch: jnp.dot(sp2s[ch], tri, preferred_element_type=F32) for ch in chains(c)}
        ws = {}
        for ch in chains(c):
            w = jnp.exp2(z2s[ch] - incls[ch] - carry[ch])
            ws[ch] = (w if ch[0] != c else jnp.where(diag_mask, w, 0.0)).astype(BF16)
            carry[ch] = carry[ch] + incls[ch][:, 0:1]
        return ws

    def accumulate(c, ws):
        vb = v_ref[blk(c), :]
        vh = [jnp.where(head_lanes[hh], vb, jnp.zeros_like(vb)).astype(BF16) for hh in range(2)]
        for r in range(c, nb):
            acc[r] = (acc[r] + jnp.dot(ws[(r, 0)], vh[0], preferred_element_type=F32)
                      + jnp.dot(ws[(r, 1)], vh[1], preferred_element_type=F32))

    pending = scores(nb - 1)
    for c in range(nb - 1, -1, -1):
        ahead = scores(c - 1) if c > 0 else None
        lg_ref[blk(c), :] = _dot_nt(h_ref[blk(c), :], wg_ref[...]).astype(lg_ref.dtype)
        accumulate(c, weights(c, *pending))
        pending = ahead
    for r in range(nb):
        o_ref[blk(r), :] = (acc[r] * _silu(z_ref[blk(r), :].astype(F32))).astype(o_ref.dtype)


def _sb_call(proj, h, w_g, layer, batch, seq):
    m, d = h.shape
    pairs = SB_HEADS * SB_HD // LANES
    cq, ck, cv, cz = (c // LANES for c in (COL_SB_Q, COL_SB_K, COL_SB_V, COL_SB_Z))
    return pl.pallas_call(
        functools.partial(_sb_kernel, tk=SB_TK),
        grid=(batch, pairs),
        in_specs=[pl.BlockSpec((seq, LANES), lambda b, p: (b, cq + p)),
                  pl.BlockSpec((seq, LANES), lambda b, p: (b, ck + p)),
                  pl.BlockSpec((seq, LANES), lambda b, p: (b, cv + p)),
                  pl.BlockSpec((seq, LANES), lambda b, p: (b, cz + p)),
                  pl.BlockSpec((seq, d), lambda b, p: (b, 0)),
                  pl.BlockSpec((None, d // pairs, d), lambda b, p: (layer, p, 0))],
        out_specs=[pl.BlockSpec((seq, LANES), lambda b, p: (b, p)),
                   pl.BlockSpec((seq, d // pairs), lambda b, p: (b, p))],
        out_shape=[jax.ShapeDtypeStruct((m, SB_HEADS * SB_HD), BRANCH_DTYPE),
                   jax.ShapeDtypeStruct((m, d), BF16)],
        compiler_params=_params("arbitrary", "arbitrary"),
        name="sb_attn",
    )(proj, proj, proj, proj, h, w_g)


def _level_masks(c):
    ri = lax.broadcasted_iota(jnp.int32, (c, c), 0)
    ci = lax.broadcasted_iota(jnp.int32, (c, c), 1)
    masks = [((ri == ci + 1) & ((ri & 1) == 1)).astype(F32)]
    b = 2
    while b < c:
        shift = int(math.log2(2 * b))
        masks.append((((ri >> shift) == (ci >> shift)) & ((ri & b) != 0) & ((ci & b) == 0)).astype(F32))
        b *= 2
    return masks


def _unit_lower_inverses(lows, masks, eye):
    invs = [eye - low * masks[0] for low in lows]
    for m in masks[1:]:
        inner = [_dot(low * m, inv) for low, inv in zip(lows, invs)]
        invs = [inv - _dot(inv, x) for inv, x in zip(invs, inner)]
    return invs


def _dn_kernel(qkv_ref, z_ref, ab_ref, cw_ref, alog_ref, dt_ref, ng_ref, o_ref,
               ext_ref, xc_ref, state_ref, *, ts):
    c = DN_CHUNK
    w_dn = DN_HEADS * DN_HD
    pad = 8

    @pl.when(pl.program_id(1) == 0)
    def _():
        ext_ref[0:pad, :] = jnp.zeros((pad, 3 * w_dn), F32)
        state_ref[...] = jnp.zeros_like(state_ref)

    @pl.when(pl.program_id(1) > 0)
    def _():
        ext_ref[0:pad, :] = ext_ref[ts:ts + pad, :]

    ext_ref[pad:pad + ts, :] = qkv_ref[...].astype(F32)

    for col0 in range(0, 3 * w_dn, LANES):
        acc = None
        for j in range(DN_CONV):
            lo = pad - (DN_CONV - 1) + j
            term = ext_ref[lo:lo + ts, col0:col0 + LANES] * cw_ref[j:j + 1, col0:col0 + LANES]
            acc = term if acc is None else acc + term
        xc_ref[:, col0:col0 + LANES] = _silu(acc)

    ri = lax.broadcasted_iota(jnp.int32, (c, c), 0)
    ci = lax.broadcasted_iota(jnp.int32, (c, c), 1)
    tril = ri >= ci
    strict = ri > ci
    tril_b = tril.astype(BF16)

    def l2n(t):
        return t * lax.rsqrt(jnp.sum(t * t, axis=-1, keepdims=True) + EPS)

    masks = _level_masks(c)
    eye = (ri == ci).astype(F32)
    heads = range(DN_HEADS)

    def local_part(ic):
        rows = slice(ic * c, (ic + 1) * c)
        ab = ab_ref[rows, :].astype(F32)
        pre = ab + dt_ref[...]
        sp = jnp.maximum(pre, 0.0) + jnp.log(1.0 + jnp.exp(-jnp.abs(pre)))
        g_all = -jnp.exp(alog_ref[...]) * sp
        beta_all = jax.nn.sigmoid(ab)
        g_hi = g_all.astype(BF16)
        g_lo = (g_all - g_hi.astype(F32)).astype(BF16)
        gc_all = (jnp.dot(tril_b, g_hi, preferred_element_type=F32)
                  + jnp.dot(tril_b, g_lo, preferred_element_type=F32))
        gc_t = gc_all.T
        qs = [l2n(xc_ref[rows, h * DN_HD:(h + 1) * DN_HD]) * (DN_HD ** -0.5) for h in heads]
        ks = [l2n(xc_ref[rows, w_dn + h * DN_HD:w_dn + (h + 1) * DN_HD]) for h in heads]
        vs = [xc_ref[rows, 2 * w_dn + h * DN_HD:2 * w_dn + (h + 1) * DN_HD] for h in heads]
        gcols = [gc_all[:, h:h + 1] for h in heads]
        betas = [beta_all[:, DN_HEADS + h:DN_HEADS + h + 1] for h in heads]
        decays = [jnp.where(tril, jnp.exp(jnp.minimum(gcols[h] - gc_t[h:h + 1, :], 0.0)), 0.0) for h in heads]
        k_betas = [ks[h] * betas[h] for h in heads]
        kqs = [_dot_nt(jnp.concatenate([k_betas[h], qs[h]], axis=0), ks[h]) for h in heads]
        lows = [jnp.where(strict, kqs[h][:c] * decays[h], 0.0) for h in heads]
        a_intras = [jnp.where(tril, kqs[h][c:] * decays[h], 0.0) for h in heads]
        egcs = [jnp.exp(gcols[h]) for h in heads]
        rhs = [jnp.concatenate([vs[h] * betas[h], k_betas[h] * egcs[h]], axis=1) for h in heads]
        q_egcs = [qs[h] * egcs[h] for h in heads]
        g_lasts = [gc_all[c - 1:c, h:h + 1] for h in heads]
        k_decs = [ks[h] * jnp.exp(g_lasts[h] - gcols[h]) for h in heads]
        e_lasts = [jnp.exp(g_lasts[h]) for h in heads]
        return lows, rhs, q_egcs, a_intras, k_decs, e_lasts

    n_chunks = ts // c
    parts = [local_part(ic) for ic in range(n_chunks)]
    t_invs = _unit_lower_inverses([low for p in parts for low in p[0]], masks, eye)
    uws = [_dot(t, r) for t, r in zip(t_invs, [r for p in parts for r in p[1]])]

    states = [state_ref[h] for h in heads]
    for ic in range(n_chunks):
        rows = slice(ic * c, (ic + 1) * c)
        _, _, q_egcs, a_intras, k_decs, e_lasts = parts[ic]
        uw = uws[ic * DN_HEADS:(ic + 1) * DN_HEADS]
        wss = [_dot(jnp.concatenate([uw[h][:, DN_HD:], q_egcs[h]], axis=0), states[h])
               for h in heads]
        v_news = [uw[h][:, :DN_HD] - wss[h][:c] for h in heads]
        states = [states[h] * e_lasts[h] + lax.dot_general(
            k_decs[h].astype(BF16), v_news[h].astype(BF16), (((0,), (0,)), ((), ())),
            preferred_element_type=F32) for h in heads]
        outs = [wss[h][c:] + _dot(a_intras[h], v_news[h]) for h in heads]
        for h in heads:
            cols = slice(h * DN_HD, (h + 1) * DN_HD)
            o_ref[rows, cols] = (_rms(outs[h], ng_ref[...])
                                 * _silu(z_ref[rows, cols].astype(F32))).astype(o_ref.dtype)
    for h in heads:
        state_ref[h] = states[h]


def _dn_call(proj, conv_w, a_log_pad, dt_pad, norm_g, batch, seq):
    m = proj.shape[0]
    ts = DN_TS
    ns = seq // ts
    w_dn = DN_HEADS * DN_HD
    return pl.pallas_call(
        functools.partial(_dn_kernel, ts=ts),
        grid=(batch, ns),
        in_specs=[pl.BlockSpec((ts, 3 * w_dn), lambda b, s: (b * ns + s, COL_DN_QKV // (3 * w_dn))),
                  pl.BlockSpec((ts, w_dn), lambda b, s: (b * ns + s, COL_DN_Z // w_dn)),
                  pl.BlockSpec((ts, LANES), lambda b, s: (b * ns + s, COL_AB // LANES)),
                  pl.BlockSpec((DN_CONV, 3 * w_dn), lambda b, s: (0, 0)),
                  pl.BlockSpec((1, LANES), lambda b, s: (0, 0)),
                  pl.BlockSpec((1, LANES), lambda b, s: (0, 0)),
                  pl.BlockSpec((1, DN_HD), lambda b, s: (0, 0))],
        out_specs=pl.BlockSpec((ts, w_dn), lambda b, s: (b * ns + s, 0)),
        out_shape=jax.ShapeDtypeStruct((m, w_dn), BRANCH_DTYPE),
        scratch_shapes=[pltpu.VMEM((ts + 8, 3 * w_dn), F32),
                        pltpu.VMEM((ts, 3 * w_dn), F32),
                        pltpu.VMEM((DN_HEADS, DN_HD, DN_HD), F32)],
        compiler_params=_params("arbitrary", "arbitrary"),
        name="deltanet",
    )(proj, proj, proj, conv_w, a_log_pad, dt_pad, norm_g)


def _rope_table_kernel(pos_ref, inv_ref, cos_ref, sin_ref):
    ang = pos_ref[0] * inv_ref[...]
    cos_ref[...] = jnp.cos(ang)
    sin_ref[...] = jnp.sin(ang)


def _rope_table_call(pos_f, inv_lane):
    batch, seq, _ = pos_f.shape
    ts = RT_TS
    ns = seq // ts
    out = jax.ShapeDtypeStruct((batch * seq, LANES), F32)
    return pl.pallas_call(
        _rope_table_kernel,
        grid=(batch, ns),
        in_specs=[pl.BlockSpec((1, ts, 1), lambda b, s: (b, s, 0)),
                  pl.BlockSpec((1, LANES), lambda b, s: (0, 0))],
        out_specs=[pl.BlockSpec((ts, LANES), lambda b, s: (b * ns + s, 0))] * 2,
        out_shape=[out, out],
        compiler_params=_params("arbitrary", "arbitrary"),
        name="rope_table",
    )(pos_f, inv_lane)


def _ret_kernel(q_ref, k_ref, v_ref, z_ref, cos_ref, sin_ref, g_ref, o_ref, state_ref, *, ts):
    c = RET_CHUNK
    w_qk = RET_HEADS * RET_QK_HD
    half = RET_QK_HD // 2

    @pl.when(pl.program_id(1) == 0)
    def _():
        state_ref[...] = jnp.zeros_like(state_ref)

    lane = lax.broadcasted_iota(jnp.int32, (1, w_qk), 1)
    first_half = (lane[:, :LANES] % RET_QK_HD) < half
    cos, sin = cos_ref[...], sin_ref[...]

    def rope(t):
        parts = []
        for l0 in range(0, w_qk, LANES):
            tl = t[:, l0:l0 + LANES]
            upper = pltpu.roll(tl, LANES - half, 1)
            lower = pltpu.roll(tl, half, 1)
            parts.append(tl * cos + jnp.where(first_half, -upper, lower) * sin)
        return jnp.concatenate(parts, axis=1)

    q = rope(q_ref[...].astype(F32)) * (RET_QK_HD ** -0.5)
    k = rope(k_ref[...].astype(F32))

    ri = lax.broadcasted_iota(jnp.int32, (c, c), 0)
    ci = lax.broadcasted_iota(jnp.int32, (c, c), 1)
    rel = (ri - ci).astype(F32)
    idx = lax.broadcasted_iota(jnp.int32, (c, 1), 0).astype(F32)

    heads = range(RET_HEADS)
    log_gammas = [math.log1p(-(2.0 ** (-5.0 - h))) for h in heads]
    intras = [jnp.where(rel >= 0.0, jnp.exp(jnp.maximum(rel, 0.0) * lg), 0.0) for lg in log_gammas]
    crosses = [jnp.exp((idx + 1.0) * lg) for lg in log_gammas]
    sdecs = [jnp.exp((c - 1.0 - idx) * lg) for lg in log_gammas]
    in_heads = [(lane >= h * RET_QK_HD) & (lane < (h + 1) * RET_QK_HD) for h in heads]
    states = [state_ref[h] for h in heads]
    for ic in range(ts // c):
        rows = slice(ic * c, (ic + 1) * c)
        kc = k[rows]
        kcb = kc.astype(BF16)
        qhs = [jnp.where(in_heads[h], q[rows], 0.0).astype(BF16) for h in heads]
        vhs = [v_ref[rows, h * RET_V_HD:(h + 1) * RET_V_HD] for h in heads]
        scores = [_dot_nt(qhs[h], kcb) * intras[h] for h in heads]
        o_cross = [_dot(qhs[h], states[h]) * crosses[h] for h in heads]
        kds = [(jnp.where(in_heads[h], kc, 0.0) * sdecs[h]).T for h in heads]
        states = [states[h] * math.exp(c * log_gammas[h]) + _dot(kds[h], vhs[h]) for h in heads]
        outs = [_dot(scores[h], vhs[h]) + o_cross[h] for h in heads]
        for h in heads:
            o = outs[h]
            mu = jnp.mean(o, axis=-1, keepdims=True)
            var = jnp.mean(jnp.square(o - mu), axis=-1, keepdims=True)
            y = (o - mu) * lax.rsqrt(var + EPS) * g_ref[:, h * RET_V_HD:(h + 1) * RET_V_HD]
            zh = z_ref[rows, h * RET_V_HD:(h + 1) * RET_V_HD].astype(F32)
            o_ref[rows, h * RET_V_HD:(h + 1) * RET_V_HD] = (y * _silu(zh)).astype(o_ref.dtype)
    for h in heads:
        state_ref[h] = states[h]


def _ret_call(proj, cos, sin, norm_g, batch, seq):
    m = proj.shape[0]
    ts = RT_TS
    ns = seq // ts
    w_qk = RET_HEADS * RET_QK_HD
    w_v = RET_HEADS * RET_V_HD
    return pl.pallas_call(
        functools.partial(_ret_kernel, ts=ts),
        grid=(batch, ns),
        in_specs=[pl.BlockSpec((ts, w_qk), lambda b, s: (b * ns + s, COL_RT_Q // w_qk)),
                  pl.BlockSpec((ts, w_qk), lambda b, s: (b * ns + s, COL_RT_K // w_qk)),
                  pl.BlockSpec((ts, w_v), lambda b, s: (b * ns + s, COL_RT_V // w_v)),
                  pl.BlockSpec((ts, w_v), lambda b, s: (b * ns + s, COL_RT_Z // w_v)),
                  pl.BlockSpec((ts, LANES), lambda b, s: (b * ns + s, 0)),
                  pl.BlockSpec((ts, LANES), lambda b, s: (b * ns + s, 0)),
                  pl.BlockSpec((1, w_v), lambda b, s: (0, 0))],
        out_specs=pl.BlockSpec((ts, w_v), lambda b, s: (b * ns + s, 0)),
        out_shape=jax.ShapeDtypeStruct((m, w_v), BRANCH_DTYPE),
        scratch_shapes=[pltpu.VMEM((RET_HEADS, w_qk, RET_V_HD), F32)],
        compiler_params=_params("arbitrary", "arbitrary"),
        name="retention",
    )(proj, proj, proj, proj, cos, sin, norm_g)


def _mem_kernel(q_ref, mem_ref, mg_ref, wkv_ref, o_ref, kv_ref):
    w_mem = MEM_HEADS * MEM_HD

    @pl.when(pl.program_id(1) == 0)
    def _():
        mem_n = _rms(mem_ref[0], mg_ref[...])
        kv_ref[...] = _dot(mem_n, wkv_ref[...])

    lane = lax.broadcasted_iota(jnp.int32, (1, w_mem), 1)
    q = q_ref[...].astype(F32) * (MEM_HD ** -0.5)
    km = kv_ref[:, 0:w_mem]
    vm = kv_ref[:, w_mem:2 * w_mem]
    heads = range(MEM_HEADS)
    in_heads = [(lane >= h * MEM_HD) & (lane < (h + 1) * MEM_HD) for h in heads]
    kmb, vmb = km.astype(BF16), vm.astype(BF16)
    scores = [_dot_nt(jnp.where(in_heads[h], q, 0.0), kmb) for h in heads]
    es = [jnp.exp(sc - jnp.max(sc, axis=-1, keepdims=True)) for sc in scores]
    ps = [e / jnp.sum(e, axis=-1, keepdims=True) for e in es]
    pvs = [_dot(p, vmb) for p in ps]
    out = pvs[0]
    for h in heads[1:]:
        out = jnp.where(in_heads[h], pvs[h], out)
    o_ref[...] = out.astype(o_ref.dtype)


def _mem_call(proj, mem, mem_g, w_kv, layer, batch, seq):
    m = proj.shape[0]
    ts = MEM_TS
    ns = seq // ts
    w_mem = MEM_HEADS * MEM_HD
    mem_len, d = mem.shape[1], mem.shape[2]
    return pl.pallas_call(
        _mem_kernel,
        grid=(batch, ns),
        in_specs=[pl.BlockSpec((ts, w_mem), lambda b, s: (b * ns + s, COL_MEM_Q // w_mem)),
                  pl.BlockSpec((1, mem_len, d), lambda b, s: (b, 0, 0)),
                  pl.BlockSpec((1, d), lambda b, s: (0, 0)),
                  pl.BlockSpec((None, d, 2 * w_mem), lambda b, s: (layer, 0, 0))],
        out_specs=pl.BlockSpec((ts, w_mem), lambda b, s: (b * ns + s, 0)),
        out_shape=jax.ShapeDtypeStruct((m, w_mem), BRANCH_DTYPE),
        scratch_shapes=[pltpu.VMEM((mem_len, 2 * w_mem), F32)],
        compiler_params=_params("arbitrary", "arbitrary"),
        name="mem_attn",
    )(proj, mem, mem_g, w_kv)


def _merge_kernel(x_ref, osb_ref, odn_ref, ort_ref, omem_ref, lg_ref, g_ref, wg_ref, bg_ref,
                  wsb_ref, wdn_ref, wrt_ref, wmem_ref, wout_ref, fg_ref, o_ref, *, final):
    d = x_ref.shape[1]
    x = x_ref[...]
    h = _rms(x, g_ref[...]).astype(BF16)
    merged = None
    branches = ((osb_ref, wsb_ref), (odn_ref, wdn_ref), (ort_ref, wrt_ref), (omem_ref, wmem_ref))
    for br, (ob_ref, wb_ref) in enumerate(branches):
        if br == 0:
            logits = lg_ref[...].astype(F32)
        else:
            logits = _dot_nt(h, wg_ref[br * d:(br + 1) * d, :])
        gate = jax.nn.sigmoid(logits + bg_ref[:, br * d:(br + 1) * d])
        term = gate * _dot(ob_ref[...], wb_ref[...])
        merged = term if merged is None else merged + term
    y = x + _dot(merged, wout_ref[...])
    if final:
        y = _rms(y, fg_ref[...])
    o_ref[...] = y


def _merge_call(xf, o_sb, o_dn, o_rt, o_mem, logits, g, w_g, b_g, w_sb, w_dn, w_rt, w_mem, w_out, fg,
                layer, final):
    m, d = xf.shape
    tm = MERGE_TM

    def rows(width):
        return pl.BlockSpec((tm, width), lambda i: (i, 0))

    def of_layer(a):
        return pl.BlockSpec((None,) + a.shape[1:], lambda i: (layer, 0, 0), pipeline_mode=pl.Buffered(1))

    args = (xf, o_sb, o_dn, o_rt, o_mem, logits, g, w_g, b_g, w_sb, w_dn, w_rt, w_mem, w_out, fg)
    in_specs = ([rows(a.shape[1]) for a in args[:6]] + [of_layer(a) for a in args[6:14]]
                + [pl.BlockSpec(fg.shape, lambda i: (0, 0), pipeline_mode=pl.Buffered(1))])
    return pl.pallas_call(
        functools.partial(_merge_kernel, final=final),
        grid=(m // tm,),
        in_specs=in_specs,
        out_specs=rows(d),
        out_shape=jax.ShapeDtypeStruct((m, d), F32),
        compiler_params=_params("arbitrary"),
        name="merge",
    )(*args)


def _arrange_kernel(w_ref, wa_ref, wg_ref):
    sb = SB_HEADS * SB_HD
    dn = DN_HEADS * DN_HD
    n_ab = 2 * DN_HEADS
    segments = ((0, 4 * sb, COL_SB_Q), (4 * sb, 3 * dn, COL_DN_QKV), (4 * sb + 3 * dn, dn, COL_DN_Z))
    src = 4 * sb + 4 * dn + n_ab
    for height, dst in ((256, COL_RT_Q), (256, COL_RT_K), (512, COL_RT_V), (512, COL_RT_Z), (256, COL_MEM_Q)):
        segments += ((src, height, dst),)
        src += height
    for s0, height, d0 in segments:
        wa_ref[d0:d0 + height, :] = w_ref[s0:s0 + height, :].astype(wa_ref.dtype)
    wa_ref[COL_AB:, :] = jnp.zeros((wa_ref.shape[0] - COL_AB, wa_ref.shape[1]), wa_ref.dtype)
    wa_ref[COL_AB:COL_AB + n_ab, :] = w_ref[4 * sb + 4 * dn:4 * sb + 4 * dn + n_ab, :].astype(wa_ref.dtype)
    wg_ref[...] = w_ref[src:, :].astype(wg_ref.dtype)


def _arrange_w_in(w):
    depth, d, n_in = w.shape
    n_gate = N_BRANCH * d
    tc = 256
    return pl.pallas_call(
        _arrange_kernel,
        grid=(depth, d // tc),
        in_specs=[pl.BlockSpec((None, n_in, tc), lambda l, j: (l, 0, j))],
        out_specs=[pl.BlockSpec((None, PROJ_COLS, tc), lambda l, j: (l, 0, j)),
                   pl.BlockSpec((None, n_gate, tc), lambda l, j: (l, 0, j))],
        out_shape=[jax.ShapeDtypeStruct((depth, PROJ_COLS, d), BF16),
                   jax.ShapeDtypeStruct((depth, n_gate, d), BF16)],
        compiler_params=_params("arbitrary", "arbitrary"),
        name="arrange_w_in",
    )(jnp.swapaxes(w, 1, 2))


def kernel(x, mem, positions, norm_g, mem_norm_g, w_in, b_gate, dn_conv_w, dn_a_log, dn_dt_bias,
           dn_norm_g, ret_norm_g, w_mem_kv, w_br_sb, w_br_dn, w_br_ret, w_br_mem, w_out, final_norm_g):
    batch, seq, d = x.shape
    depth = w_in.shape[0]
    xf = x.reshape(batch * seq, d)
    pos_f = positions.astype(F32)[..., None]
    half = RET_QK_HD // 2
    inv = ROPE_BASE ** (-jnp.arange(half, dtype=F32) / half)
    inv_lane = jnp.tile(inv, LANES // half)[None, :]
    rope_cos, rope_sin = _rope_table_call(pos_f, inv_lane)
    lane_pad = ((0, 0), (0, LANES - DN_HEADS))
    a_log_pad = jnp.pad(dn_a_log, lane_pad)
    dt_pad = jnp.pad(dn_dt_bias, lane_pad)
    w_a, w_g = _arrange_w_in(w_in)
    w_kv, w_sb, w_dn, w_rt, w_mem, w_o = (t.astype(BF16) for t in (
        w_mem_kv, w_br_sb, w_br_dn, w_br_ret, w_br_mem, w_out))
    norm_g3, b_gate3 = norm_g[:, None, :], b_gate[:, None, :]
    for l in range(depth):
        proj, h = _proj_call(xf, norm_g3, w_a, l)
        o_sb, logits = _sb_call(proj, h, w_g, l, batch, seq)
        o_dn = _dn_call(proj, dn_conv_w[l], a_log_pad[l][None, :], dt_pad[l][None, :],
                        dn_norm_g[l][None, :], batch, seq)
        o_rt = _ret_call(proj, rope_cos, rope_sin, ret_norm_g[l][None, :], batch, seq)
        o_mem = _mem_call(proj, mem, mem_norm_g[l][None, :], w_kv, l, batch, seq)
        xf = _merge_call(xf, o_sb, o_dn, o_rt, o_mem, logits, norm_g3, w_g, b_gate3, w_sb, w_dn, w_rt, w_mem,
                         w_o, final_norm_g[None, :], l, final=(l == depth - 1))
    return xf.reshape(batch, seq, d)
```

```python
import functools
import math

import jax
import jax.numpy as jnp
from jax import lax
from jax.experimental import pallas as pl
from jax.experimental.pallas import tpu as pltpu

F32 = jnp.float32
BF16 = jnp.bfloat16
EPS = 1e-6

D_MODEL = 1024
SB_HEADS, SB_HD = 8, 64
DN_HEADS, DN_HD, DN_CONV, DN_CHUNK = 4, 128, 4, 128
RET_HEADS, RET_QK_HD, RET_V_HD, RET_CHUNK = 4, 64, 128, 128
MEM_HEADS, MEM_HD = 4, 64
N_BRANCH = 4
ROPE_BASE = 10000.0
LOG2E = 1.4426950408889634

LANES = 128
VMEM_LIMIT = 56 << 20

COL_DN_QKV = 0
COL_DN_Z = 1536
COL_SB_Q = 2048
COL_SB_K = 2560
COL_SB_V = 3072
COL_SB_Z = 3584
COL_RT_V = 4096
COL_RT_Z = 4608
COL_RT_Q = 5120
COL_RT_K = 5376
COL_MEM_Q = 5632
COL_AB = 5888
PROJ_COLS = 6144

PROJ_TM, PROJ_TN = 1024, 512
SB_TK = 256
DN_TS = 512
RT_TS = 1024
MEM_TS = 1024
MERGE_TM = 1024

PROJ_DTYPE = BF16
BRANCH_DTYPE = BF16


def _params(*sem):
    return pltpu.CompilerParams(dimension_semantics=sem, vmem_limit_bytes=VMEM_LIMIT)


def _rms(x, g):
    return x * lax.rsqrt(jnp.mean(x * x, axis=-1, keepdims=True) + EPS) * g


def _silu(x):
    half = 0.5 * x
    return half + half * jnp.tanh(half)


def _dot(a, b):
    return jnp.dot(a.astype(BF16), b.astype(BF16), preferred_element_type=F32)


def _dot_nt(a, b):
    return lax.dot_general(a.astype(BF16), b.astype(BF16), (((1,), (1,)), ((), ())),
                           preferred_element_type=F32)


def _proj_kernel(x_ref, g_ref, w_ref, o_ref, h_ref):
    h = _rms(x_ref[...], g_ref[...]).astype(BF16)
    h_ref[...] = h
    n = w_ref.shape[0]
    for c0 in range(0, n, PROJ_TN):
        o_ref[:, c0:c0 + PROJ_TN] = _dot_nt(h, w_ref[c0:c0 + PROJ_TN, :]).astype(o_ref.dtype)


def _proj_call(xf, g, w_a, layer):
    m, d = xf.shape
    n = w_a.shape[1]
    return pl.pallas_call(
        _proj_kernel,
        grid=(m // PROJ_TM,),
        in_specs=[pl.BlockSpec((PROJ_TM, d), lambda i: (i, 0)),
                  pl.BlockSpec((None, 1, d), lambda i: (layer, 0, 0), pipeline_mode=pl.Buffered(1)),
                  pl.BlockSpec((None, n, d), lambda i: (layer, 0, 0), pipeline_mode=pl.Buffered(1))],
        out_specs=[pl.BlockSpec((PROJ_TM, n), lambda i: (i, 0)),
                   pl.BlockSpec((PROJ_TM, d), lambda i: (i, 0))],
        out_shape=[jax.ShapeDtypeStruct((m, n), PROJ_DTYPE),
                   jax.ShapeDtypeStruct((m, d), BF16)],
        compiler_params=_params("arbitrary"),
        name="proj",
    )(xf, g, w_a)


def _sb_kernel(q_ref, k_ref, v_ref, z_ref, h_ref, wg_ref, o_ref, lg_ref, *, tk):
    seq = q_ref.shape[0]
    nb = seq // tk
    lane = lax.broadcasted_iota(jnp.int32, (1, LANES), 1)
    row = lax.broadcasted_iota(jnp.int32, (tk, tk), 0)
    col = lax.broadcasted_iota(jnp.int32, (tk, tk), 1)
    tri = (row >= col).astype(BF16)
    diag_mask = col < row
    head_lanes = [(lane >= hh * SB_HD) & (lane < (hh + 1) * SB_HD) for hh in range(2)]

    def blk(r):
        return slice(r * tk, (r + 1) * tk)

    qms = []
    for r in range(nb):
        q = q_ref[blk(r), :].astype(F32) * (SB_HD ** -0.5 * LOG2E)
        qms.append([jnp.where(head_lanes[hh], q, 0.0).astype(BF16) for hh in range(2)])
    zero_c = jnp.zeros((tk, 1), F32)
    carry = {(r, hh): zero_c for r in range(nb) for hh in range(2)}
    acc = [jnp.zeros((tk, LANES), F32) for _ in range(nb)]

    def chains(c):
        return [(r, hh) for r in range(c, nb) for hh in range(2)]

    def scores(c):
        kb = k_ref[blk(c), :].astype(BF16)
        z2s = {ch: _dot_nt(qms[ch[0]][ch[1]], kb) for ch in chains(c)}
        sp2s = {}
        for ch, z2 in z2s.items():
            sp2 = jnp.maximum(z2, 0.0) + jnp.log2(1.0 + jnp.exp2(-jnp.abs(z2)))
            sp2s[ch] = (sp2 if ch[0] != c else jnp.where(diag_mask, sp2, 0.0)).astype(BF16)
        return z2s, sp2s

    def weights(c, z2s, sp2s):
        incls = {ch: jnp.dot(sp2s[ch], tri, preferred_element_type=F32) for ch in chains(c)}
        ws = {}
        for ch in chains(c):
            w = jnp.exp2(z2s[ch] - incls[ch] - carry[ch])
            ws[ch] = (w if ch[0] != c else jnp.where(diag_mask, w, 0.0)).astype(BF16)
            carry[ch] = carry[ch] + incls[ch][:, 0:1]
        return ws

    def accumulate(c, ws):
        vb = v_ref[blk(c), :]
        vh = [jnp.where(head_lanes[hh], vb, jnp.zeros_like(vb)).astype(BF16) for hh in range(2)]
        for r in range(c, nb):
            acc[r] = (acc[r] + jnp.dot(ws[(r, 0)], vh[0], preferred_element_type=F32)
                      + jnp.dot(ws[(r, 1)], vh[1], preferred_element_type=F32))

    pending = scores(nb - 1)
    for c in range(nb - 1, -1, -1):
        ahead = scores(c - 1) if c > 0 else None
        lg_ref[blk(c), :] = _dot_nt(h_ref[blk(c), :], wg_ref[...]).astype(lg_ref.dtype)
        accumulate(c, weights(c, *pending))
        pending = ahead
    for r in range(nb):
        o_ref[blk(r), :] = (acc[r] * _silu(z_ref[blk(r), :].astype(F32))).astype(o_ref.dtype)


def _sb_call(proj, h, w_g, layer, batch, seq):
    m, d = h.shape
    pairs = SB_HEADS * SB_HD // LANES
    cq, ck, cv, cz = (c // LANES for c in (COL_SB_Q, COL_SB_K, COL_SB_V, COL_SB_Z))
    return pl.pallas_call(
        functools.partial(_sb_kernel, tk=SB_TK),
        grid=(batch, pairs),
        in_specs=[pl.BlockSpec((seq, LANES), lambda b, p: (b, cq + p)),
                  pl.BlockSpec((seq, LANES), lambda b, p: (b, ck + p)),
                  pl.BlockSpec((seq, LANES), lambda b, p: (b, cv + p)),
                  pl.BlockSpec((seq, LANES), lambda b, p: (b, cz + p)),
                  pl.BlockSpec((seq, d), lambda b, p: (b, 0)),
                  pl.BlockSpec((None, d // pairs, d), lambda b, p: (layer, p, 0))],
        out_specs=[pl.BlockSpec((seq, LANES), lambda b, p: (b, p)),
                   pl.BlockSpec((seq, d // pairs), lambda b, p: (b, p))],
        out_shape=[jax.ShapeDtypeStruct((m, SB_HEADS * SB_HD), BRANCH_DTYPE),
                   jax.ShapeDtypeStruct((m, d), BF16)],
        compiler_params=_params("arbitrary", "arbitrary"),
        name="sb_attn",
    )(proj, proj, proj, proj, h, w_g)


def _level_masks(c):
    ri = lax.broadcasted_iota(jnp.int32, (c, c), 0)
    ci = lax.broadcasted_iota(jnp.int32, (c, c), 1)
    masks = [((ri == ci + 1) & ((ri & 1) == 1)).astype(F32)]
    b = 2
    while b < c:
        shift = int(math.log2(2 * b))
        masks.append((((ri >> shift) == (ci >> shift)) & ((ri & b) != 0) & ((ci & b) == 0)).astype(F32))
        b *= 2
    return masks


def _unit_lower_inverses(lows, masks, eye):
    invs = [eye - low * masks[0] for low in lows]
    for m in masks[1:]:
        inner = [_dot(low * m, inv) for low, inv in zip(lows, invs)]
        invs = [inv - _dot(inv, x) for inv, x in zip(invs, inner)]
    return invs


def _dn_kernel(qkv_ref, z_ref, ab_ref, cw_ref, alog_ref, dt_ref, ng_ref, o_ref,
               ext_ref, xc_ref, state_ref, *, ts):
    c = DN_CHUNK
    w_dn = DN_HEADS * DN_HD
    pad = 8

    @pl.when(pl.program_id(1) == 0)
    def _():
        ext_ref[0:pad, :] = jnp.zeros((pad, 3 * w_dn), F32)
        state_ref[...] = jnp.zeros_like(state_ref)

    @pl.when(pl.program_id(1) > 0)
    def _():
        ext_ref[0:pad, :] = ext_ref[ts:ts + pad, :]

    ext_ref[pad:pad + ts, :] = qkv_ref[...].astype(F32)

    for col0 in range(0, 3 * w_dn, LANES):
        acc = None
        for j in range(DN_CONV):
            lo = pad - (DN_CONV - 1) + j
            term = ext_ref[lo:lo + ts, col0:col0 + LANES] * cw_ref[j:j + 1, col0:col0 + LANES]
            acc = term if acc is None else acc + term
        xc_ref[:, col0:col0 + LANES] = _silu(acc)

    ri = lax.broadcasted_iota(jnp.int32, (c, c), 0)
    ci = lax.broadcasted_iota(jnp.int32, (c, c), 1)
    tril = ri >= ci
    strict = ri > ci
    tril_b = tril.astype(BF16)

    def l2n(t):
        return t * lax.rsqrt(jnp.sum(t * t, axis=-1, keepdims=True) + EPS)

    masks = _level_masks(c)
    eye = (ri == ci).astype(F32)
    heads = range(DN_HEADS)

    def local_part(ic):
        rows = slice(ic * c, (ic + 1) * c)
        ab = ab_ref[rows, :].astype(F32)
        pre = ab + dt_ref[...]
        sp = jnp.maximum(pre, 0.0) + jnp.log(1.0 + jnp.exp(-jnp.abs(pre)))
        g_all = -jnp.exp(alog_ref[...]) * sp
        beta_all = jax.nn.sigmoid(ab)
        g_hi = g_all.astype(BF16)
        g_lo = (g_all - g_hi.astype(F32)).astype(BF16)
        gc_all = (jnp.dot(tril_b, g_hi, preferred_element_type=F32)
                  + jnp.dot(tril_b, g_lo, preferred_element_type=F32))
        gc_t = gc_all.T
        qs = [l2n(xc_ref[rows, h * DN_HD:(h + 1) * DN_HD]) * (DN_HD ** -0.5) for h in heads]
        ks = [l2n(xc_ref[rows, w_dn + h * DN_HD:w_dn + (h + 1) * DN_HD]) for h in heads]
        vs = [xc_ref[rows, 2 * w_dn + h * DN_HD:2 * w_dn + (h + 1) * DN_HD] for h in heads]
        gcols = [gc_all[:, h:h + 1] for h in heads]
        betas = [beta_all[:, DN_HEADS + h:DN_HEADS + h + 1] for h in heads]
        decays = [jnp.where(tril, jnp.exp(jnp.minimum(gcols[h] - gc_t[h:h + 1, :], 0.0)), 0.0) for h in heads]
        k_betas = [ks[h] * betas[h] for h in heads]
        kqs = [_dot_nt(jnp.concatenate([k_betas[h], qs[h]], axis=0), ks[h]) for h in heads]
        lows = [jnp.where(strict, kqs[h][:c] * decays[h], 0.0) for h in heads]
        a_intras = [jnp.where(tril, kqs[h][c:] * decays[h], 0.0) for h in heads]
        egcs = [jnp.exp(gcols[h]) for h in heads]
        rhs = [jnp.concatenate([vs[h] * betas[h], k_betas[h] * egcs[h]], axis=1) for h in heads]
        q_egcs = [qs[h] * egcs[h] for h in heads]
        g_lasts = [gc_all[c - 1:c, h:h + 1] for h in heads]
        k_decs = [ks[h] * jnp.exp(g_lasts[h] - gcols[h]) for h in heads]
        e_lasts = [jnp.exp(g_lasts[h]) for h in heads]
        return lows, rhs, q_egcs, a_intras, k_decs, e_lasts

    n_chunks = ts // c
    parts = [local_part(ic) for ic in range(n_chunks)]
    t_invs = _unit_lower_inverses([low for p in parts for low in p[0]], masks, eye)
    uws = [_dot(t, r) for t, r in zip(t_invs, [r for p in parts for r in p[1]])]

    states = [state_ref[h] for h in heads]
    for ic in range(n_chunks):
        rows = slice(ic * c, (ic + 1) * c)
        _, _, q_egcs, a_intras, k_decs, e_lasts = parts[ic]
        uw = uws[ic * DN_HEADS:(ic + 1) * DN_HEADS]
        wss = [_dot(jnp.concatenate([uw[h][:, DN_HD:], q_egcs[h]], axis=0), states[h])
               for h in heads]
        v_news = [uw[h][:, :DN_HD] - wss[h][:c] for h in heads]
        states = [states[h] * e_lasts[h] + lax.dot_general(
            k_decs[h].astype(BF16), v_news[h].astype(BF16), (((0,), (0,)), ((), ())),
            preferred_element_type=F32) for h in heads]
        outs = [wss[h][c:] + _dot(a_intras[h], v_news[h]) for h in heads]
        for h in heads:
            cols = slice(h * DN_HD, (h + 1) * DN_HD)
            o_ref[rows, cols] = (_rms(outs[h], ng_ref[...])
                                 * _silu(z_ref[rows, cols].astype(F32))).astype(o_ref.dtype)
    for h in heads:
        state_ref[h] = states[h]


def _dn_call(proj, conv_w, a_log_pad, dt_pad, norm_g, batch, seq):
    m = proj.shape[0]
    ts = DN_TS
    ns = seq // ts
    w_dn = DN_HEADS * DN_HD
    return pl.pallas_call(
        functools.partial(_dn_kernel, ts=ts),
        grid=(batch, ns),
        in_specs=[pl.BlockSpec((ts, 3 * w_dn), lambda b, s: (b * ns + s, COL_DN_QKV // (3 * w_dn))),
                  pl.BlockSpec((ts, w_dn), lambda b, s: (b * ns + s, COL_DN_Z // w_dn)),
                  pl.BlockSpec((ts, LANES), lambda b, s: (b * ns + s, COL_AB // LANES)),
                  pl.BlockSpec((DN_CONV, 3 * w_dn), lambda b, s: (0, 0)),
                  pl.BlockSpec((1, LANES), lambda b, s: (0, 0)),
                  pl.BlockSpec((1, LANES), lambda b, s: (0, 0)),
                  pl.BlockSpec((1, DN_HD), lambda b, s: (0, 0))],
        out_specs=pl.BlockSpec((ts, w_dn), lambda b, s: (b * ns + s, 0)),
        out_shape=jax.ShapeDtypeStruct((m, w_dn), BRANCH_DTYPE),
        scratch_shapes=[pltpu.VMEM((ts + 8, 3 * w_dn), F32),
                        pltpu.VMEM((ts, 3 * w_dn), F32),
                        pltpu.VMEM((DN_HEADS, DN_HD, DN_HD), F32)],
        compiler_params=_params("arbitrary", "arbitrary"),
        name="deltanet",
    )(proj, proj, proj, conv_w, a_log_pad, dt_pad, norm_g)


def _rope_table_kernel(pos_ref, inv_ref, cos_ref, sin_ref):
    ang = pos_ref[0] * inv_ref[...]
    cos_ref[...] = jnp.cos(ang)
    sin_ref[...] = jnp.sin(ang)


def _rope_table_call(pos_f, inv_lane):
    batch, seq, _ = pos_f.shape
    ts = RT_TS
    ns = seq // ts
    out = jax.ShapeDtypeStruct((batch * seq, LANES), F32)
    return pl.pallas_call(
        _rope_table_kernel,
        grid=(batch, ns),
        in_specs=[pl.BlockSpec((1, ts, 1), lambda b, s: (b, s, 0)),
                  pl.BlockSpec((1, LANES), lambda b, s: (0, 0))],
        out_specs=[pl.BlockSpec((ts, LANES), lambda b, s: (b * ns + s, 0))] * 2,
        out_shape=[out, out],
        compiler_params=_params("arbitrary", "arbitrary"),
        name="rope_table",
    )(pos_f, inv_lane)


def _ret_kernel(q_ref, k_ref, v_ref, z_ref, cos_ref, sin_ref, g_ref, o_ref, state_ref, *, ts):
    c = RET_CHUNK
    w_qk = RET_HEADS * RET_QK_HD
    half = RET_QK_HD // 2

    @pl.when(pl.program_id(1) == 0)
    def _():
        state_ref[...] = jnp.zeros_like(state_ref)

    lane = lax.broadcasted_iota(jnp.int32, (1, w_qk), 1)
    first_half = (lane[:, :LANES] % RET_QK_HD) < half
    cos, sin = cos_ref[...], sin_ref[...]

    def rope(t):
        parts = []
        for l0 in range(0, w_qk, LANES):
            tl = t[:, l0:l0 + LANES]
            upper = pltpu.roll(tl, LANES - half, 1)
            lower = pltpu.roll(tl, half, 1)
            parts.append(tl * cos + jnp.where(first_half, -upper, lower) * sin)
        return jnp.concatenate(parts, axis=1)

    q = rope(q_ref[...].astype(F32)) * (RET_QK_HD ** -0.5)
    k = rope(k_ref[...].astype(F32))

    ri = lax.broadcasted_iota(jnp.int32, (c, c), 0)
    ci = lax.broadcasted_iota(jnp.int32, (c, c), 1)
    rel = (ri - ci).astype(F32)
    idx = lax.broadcasted_iota(jnp.int32, (c, 1), 0).astype(F32)

    heads = range(RET_HEADS)
    log_gammas = [math.log1p(-(2.0 ** (-5.0 - h))) for h in heads]
    intras = [jnp.where(rel >= 0.0, jnp.exp(jnp.maximum(rel, 0.0) * lg), 0.0) for lg in log_gammas]
    crosses = [jnp.exp((idx + 1.0) * lg) for lg in log_gammas]
    sdecs = [jnp.exp((c - 1.0 - idx) * lg) for lg in log_gammas]
    in_heads = [(lane >= h * RET_QK_HD) & (lane < (h + 1) * RET_QK_HD) for h in heads]
    n_chunks = ts // c
    local = []
    for ic in range(n_chunks):
        rows = slice(ic * c, (ic + 1) * c)
        kc = k[rows]
        kcb = kc.astype(BF16)
        qhs = [jnp.where(in_heads[h], q[rows], 0.0).astype(BF16) for h in heads]
        vhs = [v_ref[rows, h * RET_V_HD:(h + 1) * RET_V_HD].astype(BF16) for h in heads]
        scores = [_dot_nt(qhs[h], kcb) * intras[h] for h in heads]
        kds = [(jnp.where(in_heads[h], kc, 0.0) * sdecs[h]).T.astype(BF16) for h in heads]
        intra_out = [_dot(scores[h], vhs[h]) for h in heads]
        kvs = [_dot(kds[h], vhs[h]) for h in heads]
        local.append((qhs, intra_out, kvs))
    states = [state_ref[h] for h in heads]
    for ic in range(n_chunks):
        rows = slice(ic * c, (ic + 1) * c)
        qhs, intra_out, kvs = local[ic]
        outs = [intra_out[h] + _dot(qhs[h], states[h]) * crosses[h] for h in heads]
        states = [states[h] * math.exp(c * log_gammas[h]) + kvs[h] for h in heads]
        for h in heads:
            o = outs[h]
            mu = jnp.mean(o, axis=-1, keepdims=True)
            var = jnp.mean(jnp.square(o - mu), axis=-1, keepdims=True)
            y = (o - mu) * lax.rsqrt(var + EPS) * g_ref[:, h * RET_V_HD:(h + 1) * RET_V_HD]
            zh = z_ref[rows, h * RET_V_HD:(h + 1) * RET_V_HD].astype(F32)
            o_ref[rows, h * RET_V_HD:(h + 1) * RET_V_HD] = (y * _silu(zh)).astype(o_ref.dtype)
    for h in heads:
        state_ref[h] = states[h]


def _ret_call(proj, cos, sin, norm_g, batch, seq):
    m = proj.shape[0]
    ts = RT_TS
    ns = seq // ts
    w_qk = RET_HEADS * RET_QK_HD
    w_v = RET_HEADS * RET_V_HD
    return pl.pallas_call(
        functools.partial(_ret_kernel, ts=ts),
        grid=(batch, ns),
        in_specs=[pl.BlockSpec((ts, w_qk), lambda b, s: (b * ns + s, COL_RT_Q // w_qk)),
                  pl.BlockSpec((ts, w_qk), lambda b, s: (b * ns + s, COL_RT_K // w_qk)),
                  pl.BlockSpec((ts, w_v), lambda b, s: (b * ns + s, COL_RT_V // w_v)),
                  pl.BlockSpec((ts, w_v), lambda b, s: (b * ns + s, COL_RT_Z // w_v)),
                  pl.BlockSpec((ts, LANES), lambda b, s: (b * ns + s, 0)),
                  pl.BlockSpec((ts, LANES), lambda b, s: (b * ns + s, 0)),
                  pl.BlockSpec((1, w_v), lambda b, s: (0, 0))],
        out_specs=pl.BlockSpec((ts, w_v), lambda b, s: (b * ns + s, 0)),
        out_shape=jax.ShapeDtypeStruct((m, w_v), BRANCH_DTYPE),
        scratch_shapes=[pltpu.VMEM((RET_HEADS, w_qk, RET_V_HD), F32)],
        compiler_params=_params("arbitrary", "arbitrary"),
        name="retention",
    )(proj, proj, proj, proj, cos, sin, norm_g)


def _mem_kernel(q_ref, mem_ref, mg_ref, wkv_ref, o_ref, kv_ref):
    w_mem = MEM_HEADS * MEM_HD

    @pl.when(pl.program_id(1) == 0)
    def _():
        mem_n = _rms(mem_ref[0], mg_ref[...])
        kv_ref[...] = _dot(mem_n, wkv_ref[...])

    lane = lax.broadcasted_iota(jnp.int32, (1, w_mem), 1)
    q = q_ref[...].astype(F32) * (MEM_HD ** -0.5)
    km = kv_ref[:, 0:w_mem]
    vm = kv_ref[:, w_mem:2 * w_mem]
    heads = range(MEM_HEADS)
    in_heads = [(lane >= h * MEM_HD) & (lane < (h + 1) * MEM_HD) for h in heads]
    kmb, vmb = km.astype(BF16), vm.astype(BF16)
    scores = [_dot_nt(jnp.where(in_heads[h], q, 0.0), kmb) for h in heads]
    es = [jnp.exp(sc - jnp.max(sc, axis=-1, keepdims=True)) for sc in scores]
    ps = [e / jnp.sum(e, axis=-1, keepdims=True) for e in es]
    pvs = [_dot(p, vmb) for p in ps]
    out = pvs[0]
    for h in heads[1:]:
        out = jnp.where(in_heads[h], pvs[h], out)
    o_ref[...] = out.astype(o_ref.dtype)


def _mem_call(proj, mem, mem_g, w_kv, layer, batch, seq):
    m = proj.shape[0]
    ts = MEM_TS
    ns = seq // ts
    w_mem = MEM_HEADS * MEM_HD
    mem_len, d = mem.shape[1], mem.shape[2]
    return pl.pallas_call(
        _mem_kernel,
        grid=(batch, ns),
        in_specs=[pl.BlockSpec((ts, w_mem), lambda b, s: (b * ns + s, COL_MEM_Q // w_mem)),
                  pl.BlockSpec((1, mem_len, d), lambda b, s: (b, 0, 0)),
                  pl.BlockSpec((1, d), lambda b, s: (0, 0)),
                  pl.BlockSpec((None, d, 2 * w_mem), lambda b, s: (layer, 0, 0))],
        out_specs=pl.BlockSpec((ts, w_mem), lambda b, s: (b * ns + s, 0)),
        out_shape=jax.ShapeDtypeStruct((m, w_mem), BRANCH_DTYPE),
        scratch_shapes=[pltpu.VMEM((mem_len, 2 * w_mem), F32)],
        compiler_params=_params("arbitrary", "arbitrary"),
        name="mem_attn",
    )(proj, mem, mem_g, w_kv)


def _merge_kernel(x_ref, osb_ref, odn_ref, ort_ref, omem_ref, lg_ref, g_ref, wg_ref, bg_ref,
                  wsb_ref, wdn_ref, wrt_ref, wmem_ref, wout_ref, fg_ref, o_ref, *, final):
    d = x_ref.shape[1]
    x = x_ref[...]
    h = _rms(x, g_ref[...]).astype(BF16)
    merged = None
    branches = ((osb_ref, wsb_ref), (odn_ref, wdn_ref), (ort_ref, wrt_ref), (omem_ref, wmem_ref))
    for br, (ob_ref, wb_ref) in enumerate(branches):
        if br == 0:
            logits = lg_ref[...].astype(F32)
        else:
            logits = _dot_nt(h, wg_ref[br * d:(br + 1) * d, :])
        gate = jax.nn.sigmoid(logits + bg_ref[:, br * d:(br + 1) * d])
        term = gate * _dot(ob_ref[...], wb_ref[...])
        merged = term if merged is None else merged + term
    y = x + _dot(merged, wout_ref[...])
    if final:
        y = _rms(y, fg_ref[...])
    o_ref[...] = y


def _merge_call(xf, o_sb, o_dn, o_rt, o_mem, logits, g, w_g, b_g, w_sb, w_dn, w_rt, w_mem, w_out, fg,
                layer, final):
    m, d = xf.shape
    tm = MERGE_TM

    def rows(width):
        return pl.BlockSpec((tm, width), lambda i: (i, 0))

    def of_layer(a):
        return pl.BlockSpec((None,) + a.shape[1:], lambda i: (layer, 0, 0), pipeline_mode=pl.Buffered(1))

    args = (xf, o_sb, o_dn, o_rt, o_mem, logits, g, w_g, b_g, w_sb, w_dn, w_rt, w_mem, w_out, fg)
    in_specs = ([rows(a.shape[1]) for a in args[:6]] + [of_layer(a) for a in args[6:14]]
                + [pl.BlockSpec(fg.shape, lambda i: (0, 0), pipeline_mode=pl.Buffered(1))])
    return pl.pallas_call(
        functools.partial(_merge_kernel, final=final),
        grid=(m // tm,),
        in_specs=in_specs,
        out_specs=rows(d),
        out_shape=jax.ShapeDtypeStruct((m, d), F32),
        compiler_params=_params("arbitrary"),
        name="merge",
    )(*args)


def _arrange_kernel(w_ref, wa_ref, wg_ref):
    sb = SB_HEADS * SB_HD
    dn = DN_HEADS * DN_HD
    n_ab = 2 * DN_HEADS
    segments = ((0, 4 * sb, COL_SB_Q), (4 * sb, 3 * dn, COL_DN_QKV), (4 * sb + 3 * dn, dn, COL_DN_Z))
    src = 4 * sb + 4 * dn + n_ab
    for height, dst in ((256, COL_RT_Q), (256, COL_RT_K), (512, COL_RT_V), (512, COL_RT_Z), (256, COL_MEM_Q)):
        segments += ((src, height, dst),)
        src += height
    for s0, height, d0 in segments:
        wa_ref[d0:d0 + height, :] = w_ref[s0:s0 + height, :].astype(wa_ref.dtype)
    wa_ref[COL_AB:, :] = jnp.zeros((wa_ref.shape[0] - COL_AB, wa_ref.shape[1]), wa_ref.dtype)
    wa_ref[COL_AB:COL_AB + n_ab, :] = w_ref[4 * sb + 4 * dn:4 * sb + 4 * dn + n_ab, :].astype(wa_ref.dtype)
    wg_ref[...] = w_ref[src:, :].astype(wg_ref.dtype)


def _arrange_w_in(w):
    depth, d, n_in = w.shape
    n_gate = N_BRANCH * d
    tc = 256
    return pl.pallas_call(
        _arrange_kernel,
        grid=(depth, d // tc),
        in_specs=[pl.BlockSpec((None, n_in, tc), lambda l, j: (l, 0, j))],
        out_specs=[pl.BlockSpec((None, PROJ_COLS, tc), lambda l, j: (l, 0, j)),
                   pl.BlockSpec((None, n_gate, tc), lambda l, j: (l, 0, j))],
        out_shape=[jax.ShapeDtypeStruct((depth, PROJ_COLS, d), BF16),
                   jax.ShapeDtypeStruct((depth, n_gate, d), BF16)],
        compiler_params=_params("arbitrary", "arbitrary"),
        name="arrange_w_in",
    )(jnp.swapaxes(w, 1, 2))


def kernel(x, mem, positions, norm_g, mem_norm_g, w_in, b_gate, dn_conv_w, dn_a_log, dn_dt_bias,
           dn_norm_g, ret_norm_g, w_mem_kv, w_br_sb, w_br_dn, w_br_ret, w_br_mem, w_out, final_norm_g):
    batch, seq, d = x.shape
    depth = w_in.shape[0]
    xf = x.reshape(batch * seq, d)
    pos_f = positions.astype(F32)[..., None]
    half = RET_QK_HD // 2
    inv = ROPE_BASE ** (-jnp.arange(half, dtype=F32) / half)
    inv_lane = jnp.tile(inv, LANES // half)[None, :]
    rope_cos, rope_sin = _rope_table_call(pos_f, inv_lane)
    lane_pad = ((0, 0), (0, LANES - DN_HEADS))
    a_log_pad = jnp.pad(dn_a_log, lane_pad)
    dt_pad = jnp.pad(dn_dt_bias, lane_pad)
    w_a, w_g = _arrange_w_in(w_in)
    w_kv, w_sb, w_dn, w_rt, w_mem, w_o = (t.astype(BF16) for t in (
        w_mem_kv, w_br_sb, w_br_dn, w_br_ret, w_br_mem, w_out))
    norm_g3, b_gate3 = norm_g[:, None, :], b_gate[:, None, :]
    for l in range(depth):
        proj, h = _proj_call(xf, norm_g3, w_a, l)
        o_sb, logits = _sb_call(proj, h, w_g, l, batch, seq)
        o_dn = _dn_call(proj, dn_conv_w[l], a_log_pad[l][None, :], dt_pad[l][None, :],
                        dn_norm_g[l][None, :], batch, seq)
        o_rt = _ret_call(proj, rope_cos, rope_sin, ret_norm_g[l][None, :], batch, seq)
        o_mem = _mem_call(proj, mem, mem_norm_g[l][None, :], w_kv, l, batch, seq)
        xf = _merge_call(xf, o_sb, o_dn, o_rt, o_mem, logits, norm_g3, w_g, b_gate3, w_sb, w_dn, w_rt, w_mem,
                         w_o, final_norm_g[None, :], l, final=(l == depth - 1))
    return xf.reshape(batch, seq, d)
```

```python
import functools
import math

import jax
import jax.numpy as jnp
from jax import lax
from jax.experimental import pallas as pl
from jax.experimental.pallas import tpu as pltpu

F32 = jnp.float32
BF16 = jnp.bfloat16
EPS = 1e-6

D_MODEL = 1024
SB_HEADS, SB_HD = 8, 64
DN_HEADS, DN_HD, DN_CONV, DN_CHUNK = 4, 128, 4, 128
RET_HEADS, RET_QK_HD, RET_V_HD, RET_CHUNK = 4, 64, 128, 128
MEM_HEADS, MEM_HD = 4, 64
N_BRANCH = 4
ROPE_BASE = 10000.0
LOG2E = 1.4426950408889634

LANES = 128
V7X_VMEM_BYTES = 64 << 20
VMEM_LIMIT = V7X_VMEM_BYTES * 7 // 8

COL_DN_QKV = 0
COL_DN_Z = 1536
COL_SB_Q = 2048
COL_SB_K = 2560
COL_SB_V = 3072
COL_SB_Z = 3584
COL_RT_V = 4096
COL_RT_Z = 4608
COL_RT_Q = 5120
COL_RT_K = 5376
COL_MEM_Q = 5632
COL_AB = 5888
PROJ_COLS = 6144

PROJ_TM, PROJ_TN = 1024, 512
SB_TK = 256
DN_TS = 512
RT_TS = 1024
MEM_TS = 2048
MERGE_TM = 1024
ARRANGE_TC = 256

PROJ_DTYPE = BF16
BRANCH_DTYPE = BF16


def _params(*sem):
    return pltpu.CompilerParams(dimension_semantics=sem, vmem_limit_bytes=VMEM_LIMIT)


def _rms(x, g):
    return x * lax.rsqrt(jnp.mean(x * x, axis=-1, keepdims=True) + EPS) * g


def _silu(x):
    half = 0.5 * x
    return half + half * jnp.tanh(half)


def _dot(a, b):
    return jnp.dot(a.astype(BF16), b.astype(BF16), preferred_element_type=F32)


def _dot_nt(a, b):
    return lax.dot_general(a.astype(BF16), b.astype(BF16), (((1,), (1,)), ((), ())),
                           preferred_element_type=F32)


def _proj_kernel(x_ref, g_ref, w_ref, o_ref, h_ref):
    h = _rms(x_ref[...], g_ref[...]).astype(BF16)
    h_ref[...] = h
    n = w_ref.shape[0]
    for c0 in range(0, n, PROJ_TN):
        o_ref[:, c0:c0 + PROJ_TN] = _dot_nt(h, w_ref[c0:c0 + PROJ_TN, :]).astype(o_ref.dtype)


def _proj_call(xf, g, w_a, layer):
    m, d = xf.shape
    n = w_a.shape[1]
    return pl.pallas_call(
        _proj_kernel,
        grid=(m // PROJ_TM,),
        in_specs=[pl.BlockSpec((PROJ_TM, d), lambda i: (i, 0)),
                  pl.BlockSpec((None, 1, d), lambda i: (layer, 0, 0), pipeline_mode=pl.Buffered(1)),
                  pl.BlockSpec((None, n, d), lambda i: (layer, 0, 0), pipeline_mode=pl.Buffered(1))],
        out_specs=[pl.BlockSpec((PROJ_TM, n), lambda i: (i, 0)),
                   pl.BlockSpec((PROJ_TM, d), lambda i: (i, 0))],
        out_shape=[jax.ShapeDtypeStruct((m, n), PROJ_DTYPE),
                   jax.ShapeDtypeStruct((m, d), BF16)],
        compiler_params=_params("arbitrary"),
        name="proj",
    )(xf, g, w_a)


def _sb_kernel(q_ref, k_ref, v_ref, z_ref, h_ref, wg_ref, o_ref, lg_ref, *, tk):
    seq = q_ref.shape[0]
    nb = seq // tk
    lane = lax.broadcasted_iota(jnp.int32, (1, LANES), 1)
    row = lax.broadcasted_iota(jnp.int32, (tk, tk), 0)
    col = lax.broadcasted_iota(jnp.int32, (tk, tk), 1)
    tri = (row >= col).astype(BF16)
    diag_mask = col < row
    head_lanes = [(lane >= hh * SB_HD) & (lane < (hh + 1) * SB_HD) for hh in range(2)]

    def blk(r):
        return slice(r * tk, (r + 1) * tk)

    qms = []
    for r in range(nb):
        q = q_ref[blk(r), :].astype(F32) * (SB_HD ** -0.5 * LOG2E)
        qms.append([jnp.where(head_lanes[hh], q, 0.0).astype(BF16) for hh in range(2)])
    zero_c = jnp.zeros((tk, 1), F32)
    carry = {(r, hh): zero_c for r in range(nb) for hh in range(2)}
    acc = [jnp.zeros((tk, LANES), F32) for _ in range(nb)]

    def chains(c):
        return [(r, hh) for r in range(c, nb) for hh in range(2)]

    def scores(c):
        kb = k_ref[blk(c), :].astype(BF16)
        z2s = {ch: _dot_nt(qms[ch[0]][ch[1]], kb) for ch in chains(c)}
        sp2s = {}
        for ch, z2 in z2s.items():
            sp2 = jnp.maximum(z2, 0.0) + jnp.log2(1.0 + jnp.exp2(-jnp.abs(z2)))
            sp2s[ch] = (sp2 if ch[0] != c else jnp.where(diag_mask, sp2, 0.0)).astype(BF16)
        return z2s, sp2s

    def weights(c, z2s, sp2s):
        incls = {ch: jnp.dot(sp2s[ch], tri, preferred_element_type=F32) for ch in chains(c)}
        ws = {}
        for ch in chains(c):
            w = jnp.exp2(z2s[ch] - incls[ch] - carry[ch])
            ws[ch] = (w if ch[0] != c else jnp.where(diag_mask, w, 0.0)).astype(BF16)
            carry[ch] = carry[ch] + incls[ch][:, 0:1]
        return ws

    def accumulate(c, ws):
        vb = v_ref[blk(c), :]
        vh = [jnp.where(head_lanes[hh], vb, jnp.zeros_like(vb)).astype(BF16) for hh in range(2)]
        for r in range(c, nb):
            acc[r] = (acc[r] + jnp.dot(ws[(r, 0)], vh[0], preferred_element_type=F32)
                      + jnp.dot(ws[(r, 1)], vh[1], preferred_element_type=F32))

    pending = scores(nb - 1)
    for c in range(nb - 1, -1, -1):
        ahead = scores(c - 1) if c > 0 else None
        lg_ref[blk(c), :] = _dot_nt(h_ref[blk(c), :], wg_ref[...]).astype(lg_ref.dtype)
        accumulate(c, weights(c, *pending))
        pending = ahead
    for r in range(nb):
        o_ref[blk(r), :] = (acc[r] * _silu(z_ref[blk(r), :].astype(F32))).astype(o_ref.dtype)


def _sb_call(proj, h, w_g, layer, batch, seq):
    m, d = h.shape
    pairs = SB_HEADS * SB_HD // LANES
    cq, ck, cv, cz = (c // LANES for c in (COL_SB_Q, COL_SB_K, COL_SB_V, COL_SB_Z))
    return pl.pallas_call(
        functools.partial(_sb_kernel, tk=SB_TK),
        grid=(batch, pairs),
        in_specs=[pl.BlockSpec((seq, LANES), lambda b, p: (b, cq + p)),
                  pl.BlockSpec((seq, LANES), lambda b, p: (b, ck + p)),
                  pl.BlockSpec((seq, LANES), lambda b, p: (b, cv + p)),
                  pl.BlockSpec((seq, LANES), lambda b, p: (b, cz + p)),
                  pl.BlockSpec((seq, d), lambda b, p: (b, 0)),
                  pl.BlockSpec((None, d // pairs, d), lambda b, p: (layer, p, 0))],
        out_specs=[pl.BlockSpec((seq, LANES), lambda b, p: (b, p)),
                   pl.BlockSpec((seq, d // pairs), lambda b, p: (b, p))],
        out_shape=[jax.ShapeDtypeStruct((m, SB_HEADS * SB_HD), BRANCH_DTYPE),
                   jax.ShapeDtypeStruct((m, d), BF16)],
        compiler_params=_params("arbitrary", "arbitrary"),
        name="sb_attn",
    )(proj, proj, proj, proj, h, w_g)


def _level_masks(c):
    ri = lax.broadcasted_iota(jnp.int32, (c, c), 0)
    ci = lax.broadcasted_iota(jnp.int32, (c, c), 1)
    masks = [((ri == ci + 1) & ((ri & 1) == 1)).astype(F32)]
    b = 2
    while b < c:
        shift = int(math.log2(2 * b))
        masks.append((((ri >> shift) == (ci >> shift)) & ((ri & b) != 0) & ((ci & b) == 0)).astype(F32))
        b *= 2
    return masks


def _unit_lower_inverses(lows, masks, eye):
    invs = [eye - low * masks[0] for low in lows]
    for m in masks[1:]:
        inner = [_dot(low * m, inv) for low, inv in zip(lows, invs)]
        invs = [inv - _dot(inv, x) for inv, x in zip(invs, inner)]
    return invs


def _dn_kernel(qkv_ref, z_ref, ab_ref, cw_ref, alog_ref, dt_ref, ng_ref, o_ref,
               ext_ref, xc_ref, state_ref, *, ts):
    c = DN_CHUNK
    w_dn = DN_HEADS * DN_HD
    pad = 8

    @pl.when(pl.program_id(1) == 0)
    def _():
        ext_ref[0:pad, :] = jnp.zeros((pad, 3 * w_dn), F32)
        state_ref[...] = jnp.zeros_like(state_ref)

    @pl.when(pl.program_id(1) > 0)
    def _():
        ext_ref[0:pad, :] = ext_ref[ts:ts + pad, :]

    ext_ref[pad:pad + ts, :] = qkv_ref[...].astype(F32)

    for col0 in range(0, 3 * w_dn, LANES):
        acc = None
        for j in range(DN_CONV):
            lo = pad - (DN_CONV - 1) + j
            term = ext_ref[lo:lo + ts, col0:col0 + LANES] * cw_ref[j:j + 1, col0:col0 + LANES]
            acc = term if acc is None else acc + term
        xc_ref[:, col0:col0 + LANES] = _silu(acc)

    ri = lax.broadcasted_iota(jnp.int32, (c, c), 0)
    ci = lax.broadcasted_iota(jnp.int32, (c, c), 1)
    tril = ri >= ci
    strict = ri > ci
    tril_b = tril.astype(BF16)

    def l2n(t):
        return t * lax.rsqrt(jnp.sum(t * t, axis=-1, keepdims=True) + EPS)

    masks = _level_masks(c)
    eye = (ri == ci).astype(F32)
    heads = range(DN_HEADS)

    def local_part(ic):
        rows = slice(ic * c, (ic + 1) * c)
        ab = ab_ref[rows, :].astype(F32)
        pre = ab + dt_ref[...]
        sp = jnp.maximum(pre, 0.0) + jnp.log(1.0 + jnp.exp(-jnp.abs(pre)))
        g_all = -jnp.exp(alog_ref[...]) * sp
        beta_all = jax.nn.sigmoid(ab)
        g_hi = g_all.astype(BF16)
        g_lo = (g_all - g_hi.astype(F32)).astype(BF16)
        gc_all = (jnp.dot(tril_b, g_hi, preferred_element_type=F32)
                  + jnp.dot(tril_b, g_lo, preferred_element_type=F32))
        gc_t = gc_all.T
        qs = [l2n(xc_ref[rows, h * DN_HD:(h + 1) * DN_HD]) * (DN_HD ** -0.5) for h in heads]
        ks = [l2n(xc_ref[rows, w_dn + h * DN_HD:w_dn + (h + 1) * DN_HD]) for h in heads]
        vs = [xc_ref[rows, 2 * w_dn + h * DN_HD:2 * w_dn + (h + 1) * DN_HD] for h in heads]
        gcols = [gc_all[:, h:h + 1] for h in heads]
        betas = [beta_all[:, DN_HEADS + h:DN_HEADS + h + 1] for h in heads]
        decays = [jnp.where(tril, jnp.exp(jnp.minimum(gcols[h] - gc_t[h:h + 1, :], 0.0)), 0.0) for h in heads]
        k_betas = [ks[h] * betas[h] for h in heads]
        kqs = [_dot_nt(jnp.concatenate([k_betas[h], qs[h]], axis=0), ks[h]) for h in heads]
        lows = [jnp.where(strict, kqs[h][:c] * decays[h], 0.0) for h in heads]
        a_intras = [jnp.where(tril, kqs[h][c:] * decays[h], 0.0) for h in heads]
        egcs = [jnp.exp(gcols[h]) for h in heads]
        rhs = [jnp.concatenate([vs[h] * betas[h], k_betas[h] * egcs[h]], axis=1) for h in heads]
        q_egcs = [qs[h] * egcs[h] for h in heads]
        g_lasts = [gc_all[c - 1:c, h:h + 1] for h in heads]
        k_decs = [ks[h] * jnp.exp(g_lasts[h] - gcols[h]) for h in heads]
        e_lasts = [jnp.exp(g_lasts[h]) for h in heads]
        return lows, rhs, q_egcs, a_intras, k_decs, e_lasts

    n_chunks = ts // c
    parts = [local_part(ic) for ic in range(n_chunks)]
    t_invs = _unit_lower_inverses([low for p in parts for low in p[0]], masks, eye)
    uws = [_dot(t, r) for t, r in zip(t_invs, [r for p in parts for r in p[1]])]

    states = [state_ref[h] for h in heads]
    for ic in range(n_chunks):
        rows = slice(ic * c, (ic + 1) * c)
        _, _, q_egcs, a_intras, k_decs, e_lasts = parts[ic]
        uw = uws[ic * DN_HEADS:(ic + 1) * DN_HEADS]
        wss = [_dot(jnp.concatenate([uw[h][:, DN_HD:], q_egcs[h]], axis=0), states[h])
               for h in heads]
        v_news = [uw[h][:, :DN_HD] - wss[h][:c] for h in heads]
        states = [states[h] * e_lasts[h] + lax.dot_general(
            k_decs[h].astype(BF16), v_news[h].astype(BF16), (((0,), (0,)), ((), ())),
            preferred_element_type=F32) for h in heads]
        outs = [wss[h][c:] + _dot(a_intras[h], v_news[h]) for h in heads]
        for h in heads:
            cols = slice(h * DN_HD, (h + 1) * DN_HD)
            o_ref[rows, cols] = (_rms(outs[h], ng_ref[...])
                                 * _silu(z_ref[rows, cols].astype(F32))).astype(o_ref.dtype)
    for h in heads:
        state_ref[h] = states[h]


def _dn_call(proj, conv_w, a_log_pad, dt_pad, norm_g, batch, seq):
    m = proj.shape[0]
    ts = DN_TS
    ns = seq // ts
    w_dn = DN_HEADS * DN_HD
    return pl.pallas_call(
        functools.partial(_dn_kernel, ts=ts),
        grid=(batch, ns),
        in_specs=[pl.BlockSpec((ts, 3 * w_dn), lambda b, s: (b * ns + s, COL_DN_QKV // (3 * w_dn))),
                  pl.BlockSpec((ts, w_dn), lambda b, s: (b * ns + s, COL_DN_Z // w_dn)),
                  pl.BlockSpec((ts, LANES), lambda b, s: (b * ns + s, COL_AB // LANES)),
                  pl.BlockSpec((DN_CONV, 3 * w_dn), lambda b, s: (0, 0)),
                  pl.BlockSpec((1, LANES), lambda b, s: (0, 0)),
                  pl.BlockSpec((1, LANES), lambda b, s: (0, 0)),
                  pl.BlockSpec((1, DN_HD), lambda b, s: (0, 0))],
        out_specs=pl.BlockSpec((ts, w_dn), lambda b, s: (b * ns + s, 0)),
        out_shape=jax.ShapeDtypeStruct((m, w_dn), BRANCH_DTYPE),
        scratch_shapes=[pltpu.VMEM((ts + 8, 3 * w_dn), F32),
                        pltpu.VMEM((ts, 3 * w_dn), F32),
                        pltpu.VMEM((DN_HEADS, DN_HD, DN_HD), F32)],
        compiler_params=_params("arbitrary", "arbitrary"),
        name="deltanet",
    )(proj, proj, proj, conv_w, a_log_pad, dt_pad, norm_g)


def _rope_table_kernel(pos_ref, inv_ref, cos_ref, sin_ref):
    ang = pos_ref[0] * inv_ref[...]
    cos_ref[...] = jnp.cos(ang)
    sin_ref[...] = jnp.sin(ang)


def _rope_table_call(pos_f, inv_lane):
    batch, seq, _ = pos_f.shape
    ts = RT_TS
    ns = seq // ts
    out = jax.ShapeDtypeStruct((batch * seq, LANES), F32)
    return pl.pallas_call(
        _rope_table_kernel,
        grid=(batch, ns),
        in_specs=[pl.BlockSpec((1, ts, 1), lambda b, s: (b, s, 0)),
                  pl.BlockSpec((1, LANES), lambda b, s: (0, 0))],
        out_specs=[pl.BlockSpec((ts, LANES), lambda b, s: (b * ns + s, 0))] * 2,
        out_shape=[out, out],
        compiler_params=_params("arbitrary", "arbitrary"),
        name="rope_table",
    )(pos_f, inv_lane)


def _ret_kernel(q_ref, k_ref, v_ref, z_ref, cos_ref, sin_ref, g_ref, o_ref, state_ref, *, ts):
    c = RET_CHUNK
    w_qk = RET_HEADS * RET_QK_HD
    half = RET_QK_HD // 2

    @pl.when(pl.program_id(1) == 0)
    def _():
        state_ref[...] = jnp.zeros_like(state_ref)

    lane = lax.broadcasted_iota(jnp.int32, (1, w_qk), 1)
    first_half = (lane[:, :LANES] % RET_QK_HD) < half
    cos, sin = cos_ref[...], sin_ref[...]

    def rope(t):
        parts = []
        for l0 in range(0, w_qk, LANES):
            tl = t[:, l0:l0 + LANES]
            upper = pltpu.roll(tl, LANES - half, 1)
            lower = pltpu.roll(tl, half, 1)
            parts.append(tl * cos + jnp.where(first_half, -upper, lower) * sin)
        return jnp.concatenate(parts, axis=1)

    q = rope(q_ref[...].astype(F32)) * (RET_QK_HD ** -0.5)
    k = rope(k_ref[...].astype(F32))

    ri = lax.broadcasted_iota(jnp.int32, (c, c), 0)
    ci = lax.broadcasted_iota(jnp.int32, (c, c), 1)
    rel = (ri - ci).astype(F32)
    idx = lax.broadcasted_iota(jnp.int32, (c, 1), 0).astype(F32)

    heads = range(RET_HEADS)
    log_gammas = [math.log1p(-(2.0 ** (-5.0 - h))) for h in heads]
    intras = [jnp.where(rel >= 0.0, jnp.exp(jnp.maximum(rel, 0.0) * lg), 0.0) for lg in log_gammas]
    crosses = [jnp.exp((idx + 1.0) * lg) for lg in log_gammas]
    sdecs = [jnp.exp((c - 1.0 - idx) * lg) for lg in log_gammas]
    in_heads = [(lane >= h * RET_QK_HD) & (lane < (h + 1) * RET_QK_HD) for h in heads]
    n_chunks = ts // c
    local = []
    for ic in range(n_chunks):
        rows = slice(ic * c, (ic + 1) * c)
        kc = k[rows]
        kcb = kc.astype(BF16)
        qhs = [jnp.where(in_heads[h], q[rows], 0.0).astype(BF16) for h in heads]
        vhs = [v_ref[rows, h * RET_V_HD:(h + 1) * RET_V_HD].astype(BF16) for h in heads]
        scores = [_dot_nt(qhs[h], kcb) * intras[h] for h in heads]
        kds = [(jnp.where(in_heads[h], kc, 0.0) * sdecs[h]).T.astype(BF16) for h in heads]
        intra_out = [_dot(scores[h], vhs[h]) for h in heads]
        kvs = [_dot(kds[h], vhs[h]) for h in heads]
        local.append((qhs, intra_out, kvs))
    states = [state_ref[h] for h in heads]
    for ic in range(n_chunks):
        rows = slice(ic * c, (ic + 1) * c)
        qhs, intra_out, kvs = local[ic]
        outs = [intra_out[h] + _dot(qhs[h], states[h]) * crosses[h] for h in heads]
        states = [states[h] * math.exp(c * log_gammas[h]) + kvs[h] for h in heads]
        for h in heads:
            o = outs[h]
            mu = jnp.mean(o, axis=-1, keepdims=True)
            var = jnp.mean(jnp.square(o - mu), axis=-1, keepdims=True)
            y = (o - mu) * lax.rsqrt(var + EPS) * g_ref[:, h * RET_V_HD:(h + 1) * RET_V_HD]
            zh = z_ref[rows, h * RET_V_HD:(h + 1) * RET_V_HD].astype(F32)
            o_ref[rows, h * RET_V_HD:(h + 1) * RET_V_HD] = (y * _silu(zh)).astype(o_ref.dtype)
    for h in heads:
        state_ref[h] = states[h]


def _ret_call(proj, cos, sin, norm_g, batch, seq):
    m = proj.shape[0]
    ts = RT_TS
    ns = seq // ts
    w_qk = RET_HEADS * RET_QK_HD
    w_v = RET_HEADS * RET_V_HD
    return pl.pallas_call(
        functools.partial(_ret_kernel, ts=ts),
        grid=(batch, ns),
        in_specs=[pl.BlockSpec((ts, w_qk), lambda b, s: (b * ns + s, COL_RT_Q // w_qk)),
                  pl.BlockSpec((ts, w_qk), lambda b, s: (b * ns + s, COL_RT_K // w_qk)),
                  pl.BlockSpec((ts, w_v), lambda b, s: (b * ns + s, COL_RT_V // w_v)),
                  pl.BlockSpec((ts, w_v), lambda b, s: (b * ns + s, COL_RT_Z // w_v)),
                  pl.BlockSpec((ts, LANES), lambda b, s: (b * ns + s, 0)),
                  pl.BlockSpec((ts, LANES), lambda b, s: (b * ns + s, 0)),
                  pl.BlockSpec((1, w_v), lambda b, s: (0, 0))],
        out_specs=pl.BlockSpec((ts, w_v), lambda b, s: (b * ns + s, 0)),
        out_shape=jax.ShapeDtypeStruct((m, w_v), BRANCH_DTYPE),
        scratch_shapes=[pltpu.VMEM((RET_HEADS, w_qk, RET_V_HD), F32)],
        compiler_params=_params("arbitrary", "arbitrary"),
        name="retention",
    )(proj, proj, proj, proj, cos, sin, norm_g)


def _mem_kernel(q_ref, mem_ref, mg_ref, wkv_ref, o_ref, kv_ref):
    w_mem = MEM_HEADS * MEM_HD

    @pl.when(pl.program_id(1) == 0)
    def _():
        mem_n = _rms(mem_ref[0], mg_ref[...])
        kv_ref[...] = _dot(mem_n, wkv_ref[...])

    lane = lax.broadcasted_iota(jnp.int32, (1, w_mem), 1)
    q = q_ref[...].astype(F32) * (MEM_HD ** -0.5)
    km = kv_ref[:, 0:w_mem]
    vm = kv_ref[:, w_mem:2 * w_mem]
    heads = range(MEM_HEADS)
    in_heads = [(lane >= h * MEM_HD) & (lane < (h + 1) * MEM_HD) for h in heads]
    kmb, vmb = km.astype(BF16), vm.astype(BF16)
    scores = [_dot_nt(jnp.where(in_heads[h], q, 0.0), kmb) for h in heads]
    es = [jnp.exp(sc - jnp.max(sc, axis=-1, keepdims=True)) for sc in scores]
    ps = [e / jnp.sum(e, axis=-1, keepdims=True) for e in es]
    pvs = [_dot(p, vmb) for p in ps]
    out = pvs[0]
    for h in heads[1:]:
        out = jnp.where(in_heads[h], pvs[h], out)
    o_ref[...] = out.astype(o_ref.dtype)


def _mem_call(proj, mem, mem_g, w_kv, layer, batch, seq):
    m = proj.shape[0]
    ts = MEM_TS
    ns = seq // ts
    w_mem = MEM_HEADS * MEM_HD
    mem_len, d = mem.shape[1], mem.shape[2]
    return pl.pallas_call(
        _mem_kernel,
        grid=(batch, ns),
        in_specs=[pl.BlockSpec((ts, w_mem), lambda b, s: (b * ns + s, COL_MEM_Q // w_mem)),
                  pl.BlockSpec((1, mem_len, d), lambda b, s: (b, 0, 0)),
                  pl.BlockSpec((1, d), lambda b, s: (0, 0)),
                  pl.BlockSpec((None, d, 2 * w_mem), lambda b, s: (layer, 0, 0))],
        out_specs=pl.BlockSpec((ts, w_mem), lambda b, s: (b * ns + s, 0)),
        out_shape=jax.ShapeDtypeStruct((m, w_mem), BRANCH_DTYPE),
        scratch_shapes=[pltpu.VMEM((mem_len, 2 * w_mem), F32)],
        compiler_params=_params("arbitrary", "arbitrary"),
        name="mem_attn",
    )(proj, mem, mem_g, w_kv)


def _merge_kernel(x_ref, osb_ref, odn_ref, ort_ref, omem_ref, lg_ref, g_ref, wg_ref, bg_ref,
                  wsb_ref, wdn_ref, wrt_ref, wmem_ref, wout_ref, fg_ref, o_ref, *, final):
    d = x_ref.shape[1]
    x = x_ref[...]
    h = _rms(x, g_ref[...]).astype(BF16)
    merged = None
    branches = ((osb_ref, wsb_ref), (odn_ref, wdn_ref), (ort_ref, wrt_ref), (omem_ref, wmem_ref))
    for br, (ob_ref, wb_ref) in enumerate(branches):
        if br == 0:
            logits = lg_ref[...].astype(F32)
        else:
            logits = _dot_nt(h, wg_ref[br * d:(br + 1) * d, :])
        gate = jax.nn.sigmoid(logits + bg_ref[:, br * d:(br + 1) * d])
        term = gate * _dot(ob_ref[...], wb_ref[...])
        merged = term if merged is None else merged + term
    y = x + _dot(merged, wout_ref[...])
    if final:
        y = _rms(y, fg_ref[...])
    o_ref[...] = y


def _merge_call(xf, o_sb, o_dn, o_rt, o_mem, logits, g, w_g, b_g, w_sb, w_dn, w_rt, w_mem, w_out, fg,
                layer, final):
    m, d = xf.shape
    tm = MERGE_TM

    def rows(width):
        return pl.BlockSpec((tm, width), lambda i: (i, 0))

    def of_layer(a):
        return pl.BlockSpec((None,) + a.shape[1:], lambda i: (layer, 0, 0), pipeline_mode=pl.Buffered(1))

    args = (xf, o_sb, o_dn, o_rt, o_mem, logits, g, w_g, b_g, w_sb, w_dn, w_rt, w_mem, w_out, fg)
    in_specs = ([rows(a.shape[1]) for a in args[:6]] + [of_layer(a) for a in args[6:14]]
                + [pl.BlockSpec(fg.shape, lambda i: (0, 0), pipeline_mode=pl.Buffered(1))])
    return pl.pallas_call(
        functools.partial(_merge_kernel, final=final),
        grid=(m // tm,),
        in_specs=in_specs,
        out_specs=rows(d),
        out_shape=jax.ShapeDtypeStruct((m, d), F32),
        compiler_params=_params("arbitrary"),
        name="merge",
    )(*args)


def _arrange_kernel(w_ref, wa_ref, wg_ref):
    sb = SB_HEADS * SB_HD
    dn = DN_HEADS * DN_HD
    n_ab = 2 * DN_HEADS
    segments = ((0, 4 * sb, COL_SB_Q), (4 * sb, 3 * dn, COL_DN_QKV), (4 * sb + 3 * dn, dn, COL_DN_Z))
    src = 4 * sb + 4 * dn + n_ab
    w_qk, w_v, w_mem = RET_HEADS * RET_QK_HD, RET_HEADS * RET_V_HD, MEM_HEADS * MEM_HD
    for height, dst in ((w_qk, COL_RT_Q), (w_qk, COL_RT_K), (w_v, COL_RT_V), (w_v, COL_RT_Z), (w_mem, COL_MEM_Q)):
        segments += ((src, height, dst),)
        src += height
    for s0, height, d0 in segments:
        wa_ref[d0:d0 + height, :] = w_ref[s0:s0 + height, :].astype(wa_ref.dtype)
    wa_ref[COL_AB:, :] = jnp.zeros((wa_ref.shape[0] - COL_AB, wa_ref.shape[1]), wa_ref.dtype)
    wa_ref[COL_AB:COL_AB + n_ab, :] = w_ref[4 * sb + 4 * dn:4 * sb + 4 * dn + n_ab, :].astype(wa_ref.dtype)
    wg_ref[...] = w_ref[src:, :].astype(wg_ref.dtype)


def _arrange_w_in(w):
    depth, d, n_in = w.shape
    n_gate = N_BRANCH * d
    tc = ARRANGE_TC
    return pl.pallas_call(
        _arrange_kernel,
        grid=(depth, d // tc),
        in_specs=[pl.BlockSpec((None, n_in, tc), lambda l, j: (l, 0, j))],
        out_specs=[pl.BlockSpec((None, PROJ_COLS, tc), lambda l, j: (l, 0, j)),
                   pl.BlockSpec((None, n_gate, tc), lambda l, j: (l, 0, j))],
        out_shape=[jax.ShapeDtypeStruct((depth, PROJ_COLS, d), BF16),
                   jax.ShapeDtypeStruct((depth, n_gate, d), BF16)],
        compiler_params=_params("arbitrary", "arbitrary"),
        name="arrange_w_in",
    )(jnp.swapaxes(w, 1, 2))


def kernel(x, mem, positions, norm_g, mem_norm_g, w_in, b_gate, dn_conv_w, dn_a_log, dn_dt_bias,
           dn_norm_g, ret_norm_g, w_mem_kv, w_br_sb, w_br_dn, w_br_ret, w_br_mem, w_out, final_norm_g):
    batch, seq, d = x.shape
    depth = w_in.shape[0]
    xf = x.reshape(batch * seq, d)
    pos_f = positions.astype(F32)[..., None]
    half = RET_QK_HD // 2
    inv = ROPE_BASE ** (-jnp.arange(half, dtype=F32) / half)
    inv_lane = jnp.tile(inv, LANES // half)[None, :]
    rope_cos, rope_sin = _rope_table_call(pos_f, inv_lane)
    lane_pad = ((0, 0), (0, LANES - DN_HEADS))
    a_log_pad = jnp.pad(dn_a_log, lane_pad)
    dt_pad = jnp.pad(dn_dt_bias, lane_pad)
    w_a, w_g = _arrange_w_in(w_in)
    w_kv, w_sb, w_dn, w_rt, w_mem, w_o = (t.astype(BF16) for t in (
        w_mem_kv, w_br_sb, w_br_dn, w_br_ret, w_br_mem, w_out))
    norm_g3, b_gate3 = norm_g[:, None, :], b_gate[:, None, :]
    for l in range(depth):
        proj, h = _proj_call(xf, norm_g3, w_a, l)
        o_sb, logits = _sb_call(proj, h, w_g, l, batch, seq)
        o_dn = _dn_call(proj, dn_conv_w[l], a_log_pad[l][None, :], dt_pad[l][None, :],
                        dn_norm_g[l][None, :], batch, seq)
        o_rt = _ret_call(proj, rope_cos, rope_sin, ret_norm_g[l][None, :], batch, seq)
        o_mem = _mem_call(proj, mem, mem_norm_g[l][None, :], w_kv, l, batch, seq)
        xf = _merge_call(xf, o_sb, o_dn, o_rt, o_mem, logits, norm_g3, w_g, b_gate3, w_sb, w_dn, w_rt, w_mem,
                         w_o, final_norm_g[None, :], l, final=(l == depth - 1))
    return xf.reshape(batch, seq, d)
```

```python
import functools
import math

import jax
import jax.numpy as jnp
from jax import lax
from jax.experimental import pallas as pl
from jax.experimental.pallas import tpu as pltpu

F32 = jnp.float32
BF16 = jnp.bfloat16
EPS = 1e-6

D_MODEL = 1024
SB_HEADS, SB_HD = 8, 64
DN_HEADS, DN_HD, DN_CONV, DN_CHUNK = 4, 128, 4, 128
RET_HEADS, RET_QK_HD, RET_V_HD, RET_CHUNK = 4, 64, 128, 128
MEM_HEADS, MEM_HD = 4, 64
N_BRANCH = 4
ROPE_BASE = 10000.0
LOG2E = 1.4426950408889634

LANES = 128
V7X_VMEM_BYTES = 64 << 20
VMEM_LIMIT = V7X_VMEM_BYTES * 7 // 8

COL_DN_QKV = 0
COL_DN_Z = 1536
COL_SB_Q = 2048
COL_SB_K = 2560
COL_SB_V = 3072
COL_SB_Z = 3584
COL_RT_V = 4096
COL_RT_Z = 4608
COL_RT_Q = 5120
COL_RT_K = 5376
COL_MEM_Q = 5632
COL_AB = 5888
PROJ_COLS = 6144

PROJ_TM, PROJ_TN = 1024, 512
SB_TK = 256
DN_TS = 512
RT_TS = 1024
MEM_TS = 2048
MERGE_TM = 1024
ARRANGE_TC = 256

PROJ_DTYPE = BF16
BRANCH_DTYPE = BF16


def _params(*sem):
    return pltpu.CompilerParams(dimension_semantics=sem, vmem_limit_bytes=VMEM_LIMIT)


def _rms(x, g):
    return x * lax.rsqrt(jnp.mean(x * x, axis=-1, keepdims=True) + EPS) * g


def _silu(x):
    half = 0.5 * x
    return half + half * jnp.tanh(half)


def _dot(a, b):
    return jnp.dot(a.astype(BF16), b.astype(BF16), preferred_element_type=F32)


def _dot_nt(a, b):
    return lax.dot_general(a.astype(BF16), b.astype(BF16), (((1,), (1,)), ((), ())),
                           preferred_element_type=F32)


def _proj_kernel(x_ref, g_ref, w_ref, o_ref, h_ref):
    h = _rms(x_ref[...], g_ref[...]).astype(BF16)
    h_ref[...] = h
    n = w_ref.shape[0]
    for c0 in range(0, n, PROJ_TN):
        o_ref[:, c0:c0 + PROJ_TN] = _dot_nt(h, w_ref[c0:c0 + PROJ_TN, :]).astype(o_ref.dtype)


def _proj_call(xf, g, w_a, layer):
    m, d = xf.shape
    n = w_a.shape[1]
    return pl.pallas_call(
        _proj_kernel,
        grid=(m // PROJ_TM,),
        in_specs=[pl.BlockSpec((PROJ_TM, d), lambda i: (i, 0)),
                  pl.BlockSpec((None, 1, d), lambda i: (layer, 0, 0), pipeline_mode=pl.Buffered(1)),
                  pl.BlockSpec((None, n, d), lambda i: (layer, 0, 0), pipeline_mode=pl.Buffered(1))],
        out_specs=[pl.BlockSpec((PROJ_TM, n), lambda i: (i, 0)),
                   pl.BlockSpec((PROJ_TM, d), lambda i: (i, 0))],
        out_shape=[jax.ShapeDtypeStruct((m, n), PROJ_DTYPE),
                   jax.ShapeDtypeStruct((m, d), BF16)],
        compiler_params=_params("arbitrary"),
        name="proj",
    )(xf, g, w_a)


def _sb_kernel(q_ref, k_ref, v_ref, z_ref, h_ref, wg_ref, o_ref, lg_ref, *, tk):
    seq = q_ref.shape[0]
    nb = seq // tk
    lane = lax.broadcasted_iota(jnp.int32, (1, LANES), 1)
    row = lax.broadcasted_iota(jnp.int32, (tk, tk), 0)
    col = lax.broadcasted_iota(jnp.int32, (tk, tk), 1)
    tri = (row >= col).astype(BF16)
    diag_mask = col < row
    head_lanes = [(lane >= hh * SB_HD) & (lane < (hh + 1) * SB_HD) for hh in range(2)]

    def blk(r):
        return slice(r * tk, (r + 1) * tk)

    qms = []
    for r in range(nb):
        q = q_ref[blk(r), :].astype(F32) * (SB_HD ** -0.5 * LOG2E)
        qms.append([jnp.where(head_lanes[hh], q, 0.0).astype(BF16) for hh in range(2)])
    zero_c = jnp.zeros((tk, 1), F32)
    carry = {(r, hh): zero_c for r in range(nb) for hh in range(2)}
    acc = [jnp.zeros((tk, LANES), F32) for _ in range(nb)]

    def chains(c):
        return [(r, hh) for r in range(c, nb) for hh in range(2)]

    def scores(c):
        kb = k_ref[blk(c), :].astype(BF16)
        z2s = {ch: _dot_nt(qms[ch[0]][ch[1]], kb) for ch in chains(c)}
        sp2s = {}
        for ch, z2 in z2s.items():
            sp2 = jnp.maximum(z2, 0.0) + jnp.log2(1.0 + jnp.exp2(-jnp.abs(z2)))
            sp2s[ch] = (sp2 if ch[0] != c else jnp.where(diag_mask, sp2, 0.0)).astype(BF16)
        return z2s, sp2s

    def weights(c, z2s, sp2s):
        incls = {ch: jnp.dot(sp2s[ch], tri, preferred_element_type=F32) for ch in chains(c)}
        ws = {}
        for ch in chains(c):
            w = jnp.exp2(z2s[ch] - incls[ch] - carry[ch])
            ws[ch] = (w if ch[0] != c else jnp.where(diag_mask, w, 0.0)).astype(BF16)
            carry[ch] = carry[ch] + incls[ch][:, 0:1]
        return ws

    def accumulate(c, ws):
        vb = v_ref[blk(c), :]
        vh = [jnp.where(head_lanes[hh], vb, jnp.zeros_like(vb)).astype(BF16) for hh in range(2)]
        for r in range(c, nb):
            acc[r] = (acc[r] + jnp.dot(ws[(r, 0)], vh[0], preferred_element_type=F32)
                      + jnp.dot(ws[(r, 1)], vh[1], preferred_element_type=F32))

    pending = scores(nb - 1)
    for c in range(nb - 1, -1, -1):
        ahead = scores(c - 1) if c > 0 else None
        lg_ref[blk(c), :] = _dot_nt(h_ref[blk(c), :], wg_ref[...]).astype(lg_ref.dtype)
        accumulate(c, weights(c, *pending))
        pending = ahead
    for r in range(nb):
        o_ref[blk(r), :] = (acc[r] * _silu(z_ref[blk(r), :].astype(F32))).astype(o_ref.dtype)


def _sb_call(proj, h, w_g, layer, batch, seq):
    m, d = h.shape
    pairs = SB_HEADS * SB_HD // LANES
    cq, ck, cv, cz = (c // LANES for c in (COL_SB_Q, COL_SB_K, COL_SB_V, COL_SB_Z))
    return pl.pallas_call(
        functools.partial(_sb_kernel, tk=SB_TK),
        grid=(batch, pairs),
        in_specs=[pl.BlockSpec((seq, LANES), lambda b, p: (b, cq + p)),
                  pl.BlockSpec((seq, LANES), lambda b, p: (b, ck + p)),
                  pl.BlockSpec((seq, LANES), lambda b, p: (b, cv + p)),
                  pl.BlockSpec((seq, LANES), lambda b, p: (b, cz + p)),
                  pl.BlockSpec((seq, d), lambda b, p: (b, 0)),
                  pl.BlockSpec((None, d // pairs, d), lambda b, p: (layer, p, 0))],
        out_specs=[pl.BlockSpec((seq, LANES), lambda b, p: (b, p)),
                   pl.BlockSpec((seq, d // pairs), lambda b, p: (b, p))],
        out_shape=[jax.ShapeDtypeStruct((m, SB_HEADS * SB_HD), BRANCH_DTYPE),
                   jax.ShapeDtypeStruct((m, d), BF16)],
        compiler_params=_params("arbitrary", "arbitrary"),
        name="sb_attn",
    )(proj, proj, proj, proj, h, w_g)


def _level_masks(c):
    ri = lax.broadcasted_iota(jnp.int32, (c, c), 0)
    ci = lax.broadcasted_iota(jnp.int32, (c, c), 1)
    masks = [((ri == ci + 1) & ((ri & 1) == 1)).astype(F32)]
    b = 2
    while b < c:
        shift = int(math.log2(2 * b))
        masks.append((((ri >> shift) == (ci >> shift)) & ((ri & b) != 0) & ((ci & b) == 0)).astype(F32))
        b *= 2
    return masks


def _unit_lower_inverses(lows, masks, eye):
    invs = [eye - low * masks[0] for low in lows]
    for m in masks[1:]:
        inner = [_dot(low * m, inv) for low, inv in zip(lows, invs)]
        invs = [inv - _dot(inv, x) for inv, x in zip(invs, inner)]
    return invs


def _dn_kernel(qkv_ref, z_ref, ab_ref, cw_ref, alog_ref, dt_ref, ng_ref, o_ref,
               ext_ref, xc_ref, state_ref, *, ts):
    c = DN_CHUNK
    w_dn = DN_HEADS * DN_HD
    pad = 8
    blk = 256

    @pl.when(pl.program_id(1) == 0)
    def _():
        ext_ref[...] = jnp.zeros_like(ext_ref)
        state_ref[...] = jnp.zeros_like(state_ref)

    ri_b = lax.broadcasted_iota(jnp.int32, (blk, blk), 0)
    ci_b = lax.broadcasted_iota(jnp.int32, (blk, blk), 1)
    shifts = [(ri_b == ci_b + s).astype(BF16) for s in range(1, DN_CONV)]
    row8 = lax.broadcasted_iota(jnp.int32, (pad, 1), 0)
    for b0 in range(0, ts, blk):
        xb = qkv_ref[b0:b0 + blk, :]
        prev8 = ext_ref[...] if b0 == 0 else qkv_ref[b0 - pad:b0, :].astype(F32)
        for col0 in range(0, 3 * w_dn, 2 * LANES):
            cols = slice(col0, col0 + 2 * LANES)
            x32 = xb[:, cols].astype(F32)
            acc = x32 * cw_ref[DN_CONV - 1:DN_CONV, cols]
            head_acc = None
            for s in range(1, DN_CONV):
                shifted = jnp.dot(shifts[s - 1], xb[:, cols], preferred_element_type=F32)
                acc = acc + shifted * cw_ref[DN_CONV - 1 - s:DN_CONV - s, cols]
                wrapped = jnp.where(row8 < s, pltpu.roll(prev8[:, cols], s, 0), 0.0)
                term = wrapped * cw_ref[DN_CONV - 1 - s:DN_CONV - s, cols]
                head_acc = term if head_acc is None else head_acc + term
            xc_ref[b0:b0 + blk, cols] = _silu(acc)
            xc_ref[b0:b0 + pad, cols] = _silu(acc[:pad] + head_acc)
    ext_ref[...] = qkv_ref[ts - pad:ts, :].astype(F32)

    ri = lax.broadcasted_iota(jnp.int32, (c, c), 0)
    ci = lax.broadcasted_iota(jnp.int32, (c, c), 1)
    tril = ri >= ci
    strict = ri > ci
    tril_b = tril.astype(BF16)

    def l2n(t):
        return t * lax.rsqrt(jnp.sum(t * t, axis=-1, keepdims=True) + EPS)

    masks = _level_masks(c)
    eye = (ri == ci).astype(F32)
    heads = range(DN_HEADS)

    def local_part(ic):
        rows = slice(ic * c, (ic + 1) * c)
        ab = ab_ref[rows, :].astype(F32)
        pre = ab + dt_ref[...]
        sp = jnp.maximum(pre, 0.0) + jnp.log(1.0 + jnp.exp(-jnp.abs(pre)))
        g_all = -jnp.exp(alog_ref[...]) * sp
        beta_all = jax.nn.sigmoid(ab)
        g_hi = g_all.astype(BF16)
        g_lo = (g_all - g_hi.astype(F32)).astype(BF16)
        gc_all = (jnp.dot(tril_b, g_hi, preferred_element_type=F32)
                  + jnp.dot(tril_b, g_lo, preferred_element_type=F32))
        gc_t = gc_all.T
        qs = [l2n(xc_ref[rows, h * DN_HD:(h + 1) * DN_HD]) * (DN_HD ** -0.5) for h in heads]
        ks = [l2n(xc_ref[rows, w_dn + h * DN_HD:w_dn + (h + 1) * DN_HD]) for h in heads]
        vs = [xc_ref[rows, 2 * w_dn + h * DN_HD:2 * w_dn + (h + 1) * DN_HD] for h in heads]
        gcols = [gc_all[:, h:h + 1] for h in heads]
        betas = [beta_all[:, DN_HEADS + h:DN_HEADS + h + 1] for h in heads]
        decays = [jnp.where(tril, jnp.exp(jnp.minimum(gcols[h] - gc_t[h:h + 1, :], 0.0)), 0.0) for h in heads]
        k_betas = [ks[h] * betas[h] for h in heads]
        kqs = [_dot_nt(jnp.concatenate([k_betas[h], qs[h]], axis=0), ks[h]) for h in heads]
        lows = [jnp.where(strict, kqs[h][:c] * decays[h], 0.0) for h in heads]
        a_intras = [jnp.where(tril, kqs[h][c:] * decays[h], 0.0) for h in heads]
        egcs = [jnp.exp(gcols[h]) for h in heads]
        rhs = [jnp.concatenate([vs[h] * betas[h], k_betas[h] * egcs[h]], axis=1) for h in heads]
        q_egcs = [qs[h] * egcs[h] for h in heads]
        g_lasts = [gc_all[c - 1:c, h:h + 1] for h in heads]
        k_decs = [ks[h] * jnp.exp(g_lasts[h] - gcols[h]) for h in heads]
        e_lasts = [jnp.exp(g_lasts[h]) for h in heads]
        return lows, rhs, q_egcs, a_intras, k_decs, e_lasts

    n_chunks = ts // c
    parts = [local_part(ic) for ic in range(n_chunks)]
    t_invs = _unit_lower_inverses([low for p in parts for low in p[0]], masks, eye)
    uws = [_dot(t, r) for t, r in zip(t_invs, [r for p in parts for r in p[1]])]

    states = [state_ref[h] for h in heads]
    for ic in range(n_chunks):
        rows = slice(ic * c, (ic + 1) * c)
        _, _, q_egcs, a_intras, k_decs, e_lasts = parts[ic]
        uw = uws[ic * DN_HEADS:(ic + 1) * DN_HEADS]
        wss = [_dot(jnp.concatenate([uw[h][:, DN_HD:], q_egcs[h]], axis=0), states[h])
               for h in heads]
        v_news = [uw[h][:, :DN_HD] - wss[h][:c] for h in heads]
        states = [states[h] * e_lasts[h] + lax.dot_general(
            k_decs[h].astype(BF16), v_news[h].astype(BF16), (((0,), (0,)), ((), ())),
            preferred_element_type=F32) for h in heads]
        outs = [wss[h][c:] + _dot(a_intras[h], v_news[h]) for h in heads]
        for h in heads:
            cols = slice(h * DN_HD, (h + 1) * DN_HD)
            o_ref[rows, cols] = (_rms(outs[h], ng_ref[...])
                                 * _silu(z_ref[rows, cols].astype(F32))).astype(o_ref.dtype)
    for h in heads:
        state_ref[h] = states[h]


def _dn_call(proj, conv_w, a_log_pad, dt_pad, norm_g, batch, seq):
    m = proj.shape[0]
    ts = DN_TS
    ns = seq // ts
    w_dn = DN_HEADS * DN_HD
    return pl.pallas_call(
        functools.partial(_dn_kernel, ts=ts),
        grid=(batch, ns),
        in_specs=[pl.BlockSpec((ts, 3 * w_dn), lambda b, s: (b * ns + s, COL_DN_QKV // (3 * w_dn))),
                  pl.BlockSpec((ts, w_dn), lambda b, s: (b * ns + s, COL_DN_Z // w_dn)),
                  pl.BlockSpec((ts, LANES), lambda b, s: (b * ns + s, COL_AB // LANES)),
                  pl.BlockSpec((DN_CONV, 3 * w_dn), lambda b, s: (0, 0)),
                  pl.BlockSpec((1, LANES), lambda b, s: (0, 0)),
                  pl.BlockSpec((1, LANES), lambda b, s: (0, 0)),
                  pl.BlockSpec((1, DN_HD), lambda b, s: (0, 0))],
        out_specs=pl.BlockSpec((ts, w_dn), lambda b, s: (b * ns + s, 0)),
        out_shape=jax.ShapeDtypeStruct((m, w_dn), BRANCH_DTYPE),
        scratch_shapes=[pltpu.VMEM((8, 3 * w_dn), F32),
                        pltpu.VMEM((ts, 3 * w_dn), F32),
                        pltpu.VMEM((DN_HEADS, DN_HD, DN_HD), F32)],
        compiler_params=_params("arbitrary", "arbitrary"),
        name="deltanet",
    )(proj, proj, proj, conv_w, a_log_pad, dt_pad, norm_g)


def _rope_table_kernel(pos_ref, inv_ref, cos_ref, sin_ref):
    ang = pos_ref[0] * inv_ref[...]
    cos_ref[...] = jnp.cos(ang)
    sin_ref[...] = jnp.sin(ang)


def _rope_table_call(pos_f, inv_lane):
    batch, seq, _ = pos_f.shape
    ts = RT_TS
    ns = seq // ts
    out = jax.ShapeDtypeStruct((batch * seq, LANES), F32)
    return pl.pallas_call(
        _rope_table_kernel,
        grid=(batch, ns),
        in_specs=[pl.BlockSpec((1, ts, 1), lambda b, s: (b, s, 0)),
                  pl.BlockSpec((1, LANES), lambda b, s: (0, 0))],
        out_specs=[pl.BlockSpec((ts, LANES), lambda b, s: (b * ns + s, 0))] * 2,
        out_shape=[out, out],
        compiler_params=_params("arbitrary", "arbitrary"),
        name="rope_table",
    )(pos_f, inv_lane)


def _ret_kernel(q_ref, k_ref, v_ref, z_ref, cos_ref, sin_ref, g_ref, o_ref, state_ref, *, ts):
    c = RET_CHUNK
    w_qk = RET_HEADS * RET_QK_HD
    half = RET_QK_HD // 2

    @pl.when(pl.program_id(1) == 0)
    def _():
        state_ref[...] = jnp.zeros_like(state_ref)

    lane = lax.broadcasted_iota(jnp.int32, (1, w_qk), 1)
    first_half = (lane[:, :LANES] % RET_QK_HD) < half
    cos, sin = cos_ref[...], sin_ref[...]

    def rope(t):
        parts = []
        for l0 in range(0, w_qk, LANES):
            tl = t[:, l0:l0 + LANES]
            upper = pltpu.roll(tl, LANES - half, 1)
            lower = pltpu.roll(tl, half, 1)
            parts.append(tl * cos + jnp.where(first_half, -upper, lower) * sin)
        return jnp.concatenate(parts, axis=1)

    q = rope(q_ref[...].astype(F32)) * (RET_QK_HD ** -0.5)
    k = rope(k_ref[...].astype(F32))

    ri = lax.broadcasted_iota(jnp.int32, (c, c), 0)
    ci = lax.broadcasted_iota(jnp.int32, (c, c), 1)
    rel = (ri - ci).astype(F32)
    idx = lax.broadcasted_iota(jnp.int32, (c, 1), 0).astype(F32)

    heads = range(RET_HEADS)
    log_gammas = [math.log1p(-(2.0 ** (-5.0 - h))) for h in heads]
    intras = [jnp.where(rel >= 0.0, jnp.exp(jnp.maximum(rel, 0.0) * lg), 0.0) for lg in log_gammas]
    crosses = [jnp.exp((idx + 1.0) * lg) for lg in log_gammas]
    sdecs = [jnp.exp((c - 1.0 - idx) * lg) for lg in log_gammas]
    in_heads = [(lane >= h * RET_QK_HD) & (lane < (h + 1) * RET_QK_HD) for h in heads]
    n_chunks = ts // c
    local = []
    for ic in range(n_chunks):
        rows = slice(ic * c, (ic + 1) * c)
        kc = k[rows]
        kcb = kc.astype(BF16)
        qhs = [jnp.where(in_heads[h], q[rows], 0.0).astype(BF16) for h in heads]
        vhs = [v_ref[rows, h * RET_V_HD:(h + 1) * RET_V_HD].astype(BF16) for h in heads]
        scores = [_dot_nt(qhs[h], kcb) * intras[h] for h in heads]
        kds = [(jnp.where(in_heads[h], kc, 0.0) * sdecs[h]).T.astype(BF16) for h in heads]
        intra_out = [_dot(scores[h], vhs[h]) for h in heads]
        kvs = [_dot(kds[h], vhs[h]) for h in heads]
        local.append((qhs, intra_out, kvs))
    states = [state_ref[h] for h in heads]
    for ic in range(n_chunks):
        rows = slice(ic * c, (ic + 1) * c)
        qhs, intra_out, kvs = local[ic]
        outs = [intra_out[h] + _dot(qhs[h], states[h]) * crosses[h] for h in heads]
        states = [states[h] * math.exp(c * log_gammas[h]) + kvs[h] for h in heads]
        for h in heads:
            o = outs[h]
            mu = jnp.mean(o, axis=-1, keepdims=True)
            var = jnp.mean(jnp.square(o - mu), axis=-1, keepdims=True)
            y = (o - mu) * lax.rsqrt(var + EPS) * g_ref[:, h * RET_V_HD:(h + 1) * RET_V_HD]
            zh = z_ref[rows, h * RET_V_HD:(h + 1) * RET_V_HD].astype(F32)
            o_ref[rows, h * RET_V_HD:(h + 1) * RET_V_HD] = (y * _silu(zh)).astype(o_ref.dtype)
    for h in heads:
        state_ref[h] = states[h]


def _ret_call(proj, cos, sin, norm_g, batch, seq):
    m = proj.shape[0]
    ts = RT_TS
    ns = seq // ts
    w_qk = RET_HEADS * RET_QK_HD
    w_v = RET_HEADS * RET_V_HD
    return pl.pallas_call(
        functools.partial(_ret_kernel, ts=ts),
        grid=(batch, ns),
        in_specs=[pl.BlockSpec((ts, w_qk), lambda b, s: (b * ns + s, COL_RT_Q // w_qk)),
                  pl.BlockSpec((ts, w_qk), lambda b, s: (b * ns + s, COL_RT_K // w_qk)),
                  pl.BlockSpec((ts, w_v), lambda b, s: (b * ns + s, COL_RT_V // w_v)),
                  pl.BlockSpec((ts, w_v), lambda b, s: (b * ns + s, COL_RT_Z // w_v)),
                  pl.BlockSpec((ts, LANES), lambda b, s: (b * ns + s, 0)),
                  pl.BlockSpec((ts, LANES), lambda b, s: (b * ns + s, 0)),
                  pl.BlockSpec((1, w_v), lambda b, s: (0, 0))],
        out_specs=pl.BlockSpec((ts, w_v), lambda b, s: (b * ns + s, 0)),
        out_shape=jax.ShapeDtypeStruct((m, w_v), BRANCH_DTYPE),
        scratch_shapes=[pltpu.VMEM((RET_HEADS, w_qk, RET_V_HD), F32)],
        compiler_params=_params("arbitrary", "arbitrary"),
        name="retention",
    )(proj, proj, proj, proj, cos, sin, norm_g)


def _mem_kernel(q_ref, mem_ref, mg_ref, wkv_ref, o_ref, kv_ref):
    w_mem = MEM_HEADS * MEM_HD

    @pl.when(pl.program_id(1) == 0)
    def _():
        mem_n = _rms(mem_ref[0], mg_ref[...])
        kv_ref[...] = _dot(mem_n, wkv_ref[...])

    lane = lax.broadcasted_iota(jnp.int32, (1, w_mem), 1)
    q = q_ref[...].astype(F32) * (MEM_HD ** -0.5)
    km = kv_ref[:, 0:w_mem]
    vm = kv_ref[:, w_mem:2 * w_mem]
    heads = range(MEM_HEADS)
    in_heads = [(lane >= h * MEM_HD) & (lane < (h + 1) * MEM_HD) for h in heads]
    kmb, vmb = km.astype(BF16), vm.astype(BF16)
    scores = [_dot_nt(jnp.where(in_heads[h], q, 0.0), kmb) for h in heads]
    es = [jnp.exp(sc - jnp.max(sc, axis=-1, keepdims=True)) for sc in scores]
    ps = [e / jnp.sum(e, axis=-1, keepdims=True) for e in es]
    pvs = [_dot(p, vmb) for p in ps]
    out = pvs[0]
    for h in heads[1:]:
        out = jnp.where(in_heads[h], pvs[h], out)
    o_ref[...] = out.astype(o_ref.dtype)


def _mem_call(proj, mem, mem_g, w_kv, layer, batch, seq):
    m = proj.shape[0]
    ts = MEM_TS
    ns = seq // ts
    w_mem = MEM_HEADS * MEM_HD
    mem_len, d = mem.shape[1], mem.shape[2]
    return pl.pallas_call(
        _mem_kernel,
        grid=(batch, ns),
        in_specs=[pl.BlockSpec((ts, w_mem), lambda b, s: (b * ns + s, COL_MEM_Q // w_mem)),
                  pl.BlockSpec((1, mem_len, d), lambda b, s: (b, 0, 0)),
                  pl.BlockSpec((1, d), lambda b, s: (0, 0)),
                  pl.BlockSpec((None, d, 2 * w_mem), lambda b, s: (layer, 0, 0))],
        out_specs=pl.BlockSpec((ts, w_mem), lambda b, s: (b * ns + s, 0)),
        out_shape=jax.ShapeDtypeStruct((m, w_mem), BRANCH_DTYPE),
        scratch_shapes=[pltpu.VMEM((mem_len, 2 * w_mem), F32)],
        compiler_params=_params("arbitrary", "arbitrary"),
        name="mem_attn",
    )(proj, mem, mem_g, w_kv)


def _merge_kernel(x_ref, osb_ref, odn_ref, ort_ref, omem_ref, lg_ref, g_ref, wg_ref, bg_ref,
                  wsb_ref, wdn_ref, wrt_ref, wmem_ref, wout_ref, fg_ref, o_ref, *, final):
    d = x_ref.shape[1]
    x = x_ref[...]
    h = _rms(x, g_ref[...]).astype(BF16)
    merged = None
    branches = ((osb_ref, wsb_ref), (odn_ref, wdn_ref), (ort_ref, wrt_ref), (omem_ref, wmem_ref))
    for br, (ob_ref, wb_ref) in enumerate(branches):
        if br == 0:
            logits = lg_ref[...].astype(F32)
        else:
            logits = _dot_nt(h, wg_ref[br * d:(br + 1) * d, :])
        gate = jax.nn.sigmoid(logits + bg_ref[:, br * d:(br + 1) * d])
        term = gate * _dot(ob_ref[...], wb_ref[...])
        merged = term if merged is None else merged + term
    y = x + _dot(merged, wout_ref[...])
    if final:
        y = _rms(y, fg_ref[...])
    o_ref[...] = y


def _merge_call(xf, o_sb, o_dn, o_rt, o_mem, logits, g, w_g, b_g, w_sb, w_dn, w_rt, w_mem, w_out, fg,
                layer, final):
    m, d = xf.shape
    tm = MERGE_TM

    def rows(width):
        return pl.BlockSpec((tm, width), lambda i: (i, 0))

    def of_layer(a):
        return pl.BlockSpec((None,) + a.shape[1:], lambda i: (layer, 0, 0), pipeline_mode=pl.Buffered(1))

    args = (xf, o_sb, o_dn, o_rt, o_mem, logits, g, w_g, b_g, w_sb, w_dn, w_rt, w_mem, w_out, fg)
    in_specs = ([rows(a.shape[1]) for a in args[:6]] + [of_layer(a) for a in args[6:14]]
                + [pl.BlockSpec(fg.shape, lambda i: (0, 0), pipeline_mode=pl.Buffered(1))])
    return pl.pallas_call(
        functools.partial(_merge_kernel, final=final),
        grid=(m // tm,),
        in_specs=in_specs,
        out_specs=rows(d),
        out_shape=jax.ShapeDtypeStruct((m, d), F32),
        compiler_params=_params("arbitrary"),
        name="merge",
    )(*args)


def _arrange_kernel(w_ref, wa_ref, wg_ref):
    sb = SB_HEADS * SB_HD
    dn = DN_HEADS * DN_HD
    n_ab = 2 * DN_HEADS
    segments = ((0, 4 * sb, COL_SB_Q), (4 * sb, 3 * dn, COL_DN_QKV), (4 * sb + 3 * dn, dn, COL_DN_Z))
    src = 4 * sb + 4 * dn + n_ab
    w_qk, w_v, w_mem = RET_HEADS * RET_QK_HD, RET_HEADS * RET_V_HD, MEM_HEADS * MEM_HD
    for height, dst in ((w_qk, COL_RT_Q), (w_qk, COL_RT_K), (w_v, COL_RT_V), (w_v, COL_RT_Z), (w_mem, COL_MEM_Q)):
        segments += ((src, height, dst),)
        src += height
    for s0, height, d0 in segments:
        wa_ref[d0:d0 + height, :] = w_ref[s0:s0 + height, :].astype(wa_ref.dtype)
    wa_ref[COL_AB:, :] = jnp.zeros((wa_ref.shape[0] - COL_AB, wa_ref.shape[1]), wa_ref.dtype)
    wa_ref[COL_AB:COL_AB + n_ab, :] = w_ref[4 * sb + 4 * dn:4 * sb + 4 * dn + n_ab, :].astype(wa_ref.dtype)
    wg_ref[...] = w_ref[src:, :].astype(wg_ref.dtype)


def _arrange_w_in(w):
    depth, d, n_in = w.shape
    n_gate = N_BRANCH * d
    tc = ARRANGE_TC
    return pl.pallas_call(
        _arrange_kernel,
        grid=(depth, d // tc),
        in_specs=[pl.BlockSpec((None, n_in, tc), lambda l, j: (l, 0, j))],
        out_specs=[pl.BlockSpec((None, PROJ_COLS, tc), lambda l, j: (l, 0, j)),
                   pl.BlockSpec((None, n_gate, tc), lambda l, j: (l, 0, j))],
        out_shape=[jax.ShapeDtypeStruct((depth, PROJ_COLS, d), BF16),
                   jax.ShapeDtypeStruct((depth, n_gate, d), BF16)],
        compiler_params=_params("arbitrary", "arbitrary"),
        name="arrange_w_in",
    )(jnp.swapaxes(w, 1, 2))


def kernel(x, mem, positions, norm_g, mem_norm_g, w_in, b_gate, dn_conv_w, dn_a_log, dn_dt_bias,
           dn_norm_g, ret_norm_g, w_mem_kv, w_br_sb, w_br_dn, w_br_ret, w_br_mem, w_out, final_norm_g):
    batch, seq, d = x.shape
    depth = w_in.shape[0]
    xf = x.reshape(batch * seq, d)
    pos_f = positions.astype(F32)[..., None]
    half = RET_QK_HD // 2
    inv = ROPE_BASE ** (-jnp.arange(half, dtype=F32) / half)
    inv_lane = jnp.tile(inv, LANES // half)[None, :]
    rope_cos, rope_sin = _rope_table_call(pos_f, inv_lane)
    lane_pad = ((0, 0), (0, LANES - DN_HEADS))
    a_log_pad = jnp.pad(dn_a_log, lane_pad)
    dt_pad = jnp.pad(dn_dt_bias, lane_pad)
    w_a, w_g = _arrange_w_in(w_in)
    w_kv, w_sb, w_dn, w_rt, w_mem, w_o = (t.astype(BF16) for t in (
        w_mem_kv, w_br_sb, w_br_dn, w_br_ret, w_br_mem, w_out))
    norm_g3, b_gate3 = norm_g[:, None, :], b_gate[:, None, :]
    for l in range(depth):
        proj, h = _proj_call(xf, norm_g3, w_a, l)
        o_sb, logits = _sb_call(proj, h, w_g, l, batch, seq)
        o_dn = _dn_call(proj, dn_conv_w[l], a_log_pad[l][None, :], dt_pad[l][None, :],
                        dn_norm_g[l][None, :], batch, seq)
        o_rt = _ret_call(proj, rope_cos, rope_sin, ret_norm_g[l][None, :], batch, seq)
        o_mem = _mem_call(proj, mem, mem_norm_g[l][None, :], w_kv, l, batch, seq)
        xf = _merge_call(xf, o_sb, o_dn, o_rt, o_mem, logits, norm_g3, w_g, b_gate3, w_sb, w_dn, w_rt, w_mem,
                         w_o, final_norm_g[None, :], l, final=(l == depth - 1))
    return xf.reshape(batch, seq, d)
```
